```python
import math
import jax, jax.numpy as jnp
from jax import lax
import numpy as np

D_MODEL = 1024
BATCH = 2
SEQ = 8192
DEPTH = 4

GRID_W = 64
HEAD_DIM = 64
ROPE_THETA = 10000.0
NEG_INF = -1e30
LN_EPS = 1e-5
A_HEADS = 8
A_KV_HEADS = 2
A_WINDOW = 128
A_BLOCK = 128
B_HEADS = 8
B_KH = 8
B_KW = 16
B_QW = 16
C_HEADS = 4
C_VDIM = 2 * HEAD_DIM
C_BLOCK = 128
D_WIDTH = 512
D_BLOCKS = 8
D_CONV = 4
LRU_C = 8.0
N_BRANCH = 4
BRANCH_W = 512
PEER_HEADS = 8
PEER_KEYS = 128
PEER_N = PEER_KEYS * PEER_KEYS
PEER_QDIM = 256
PEER_TOPK = 16
PEER_CHUNK = 128
ALPHA = (2.0 * DEPTH) ** 0.25
BETA = (8.0 * DEPTH) ** -0.25
A_Q = A_HEADS * HEAD_DIM
A_KV = A_KV_HEADS * HEAD_DIM
B_QKV = B_HEADS * HEAD_DIM
C_QK = C_HEADS * 2 * HEAD_DIM
C_V = C_HEADS * C_VDIM
GATE_COLS = N_BRANCH * D_MODEL
IN_SPLITS = (A_Q, A_KV, A_KV, B_QKV, B_QKV, B_QKV, C_QK, C_QK, C_V, D_WIDTH, D_WIDTH, GATE_COLS)
IN_COLS = A_Q + 2 * A_KV + 3 * B_QKV + 2 * C_QK + C_V + 2 * D_WIDTH + GATE_COLS

kernel_name = 'hybrid_gated_encoder'


def layer_norm(x, g, b):
    xf = x.astype(jnp.float32)
    mu = jnp.mean(xf, axis=-1, keepdims=True)
    var = jnp.mean(jnp.square(xf - mu), axis=-1, keepdims=True)
    return ((xf - mu) * lax.rsqrt(var + LN_EPS) * g.astype(jnp.float32) + b.astype(jnp.float32)).astype(x.dtype)


def rope_tables(seq):
    inv = 1.0 / (ROPE_THETA ** (jnp.arange(0, HEAD_DIM, 2, dtype=jnp.float32) / HEAD_DIM))
    ang = jnp.arange(seq, dtype=jnp.float32)[:, None] * inv[None, :]
    return jnp.cos(ang), jnp.sin(ang)


def apply_rope(x, cos, sin):
    x1, x2 = jnp.split(x, 2, axis=-1)
    shape = (1, cos.shape[0]) + (1,) * (x.ndim - 3) + (cos.shape[1],)
    c = cos.reshape(shape).astype(x.dtype)
    s = sin.reshape(shape).astype(x.dtype)
    return jnp.concatenate([x1 * c - x2 * s, x2 * c + x1 * s], axis=-1)


def window_gqa(q, k, v, sink, cos, sin):
    bsz, seq = q.shape[:2]
    nb = seq // A_BLOCK
    grp = A_HEADS // A_KV_HEADS
    q = apply_rope(q, cos, sin) * (HEAD_DIM ** -0.5)
    k = apply_rope(k, cos, sin)
    pad = ((0, 0), (A_BLOCK, A_BLOCK), (0, 0), (0, 0))
    kp = jnp.pad(k, pad).reshape(bsz, nb + 2, A_BLOCK, A_KV_HEADS, HEAD_DIM)
    vp = jnp.pad(v, pad).reshape(bsz, nb + 2, A_BLOCK, A_KV_HEADS, HEAD_DIM)
    kw = jnp.concatenate([kp[:, :-2], kp[:, 1:-1], kp[:, 2:]], axis=2)
    vw = jnp.concatenate([vp[:, :-2], vp[:, 1:-1], vp[:, 2:]], axis=2)
    qb = q.reshape(bsz, nb, A_BLOCK, A_KV_HEADS, grp, HEAD_DIM)
    s = jnp.einsum('bnqkgd,bnjkd->bnkgqj', qb, kw).astype(jnp.float32)
    qi = np.arange(A_BLOCK)[:, None]
    kj = np.arange(3 * A_BLOCK)[None, :]
    rel = kj - A_BLOCK - qi
    kpos = np.arange(nb)[:, None, None] * A_BLOCK + kj[None] - A_BLOCK
    valid = (np.abs(rel)[None] <= A_WINDOW) & (kpos >= 0) & (kpos < seq)
    s = jnp.where(valid[None, :, None, None], s, NEG_INF)
    sk = sink.astype(jnp.float32).reshape(A_KV_HEADS, grp)[None, None, :, :, None, None]
    m = jnp.maximum(jnp.max(s, axis=-1, keepdims=True), sk)
    p = jnp.exp(s - m)
    p = (p / (jnp.sum(p, axis=-1, keepdims=True) + jnp.exp(sk - m))).astype(v.dtype)
    o = jnp.einsum('bnkgqj,bnjkd->bnqkgd', p, vw)
    return o.reshape(bsz, seq, A_HEADS * HEAD_DIM)


def neighbourhood_attention(q, k, v, rpb):
    bsz, seq = q.shape[:2]
    rows = seq // GRID_W
    kh = min(B_KH, rows)
    kcw = B_QW + B_KW
    qg = (q * HEAD_DIM ** -0.5).reshape(bsz, rows, GRID_W, B_HEADS, HEAD_DIM)
    kg = k.reshape(bsz, rows, GRID_W, B_HEADS, HEAD_DIM)
    vg = v.reshape(bsz, rows, GRID_W, B_HEADS, HEAD_DIM)
    r = np.arange(rows)
    row_idx = np.clip(r - B_KH // 2, 0, rows - kh)[:, None] + np.arange(kh)[None, :]
    drow = row_idx - r[:, None] + (B_KH - 1)
    outs = []
    for c0 in range(0, GRID_W, B_QW):
        cs = int(np.clip(c0 - B_KW // 2, 0, GRID_W - kcw))
        qcol = np.arange(c0, c0 + B_QW)
        kcol = np.arange(cs, cs + kcw)
        cstart = np.clip(qcol - B_KW // 2, 0, GRID_W - B_KW)
        cvalid = (kcol[None, :] >= cstart[:, None]) & (kcol[None, :] < cstart[:, None] + B_KW)
        mask = np.broadcast_to(cvalid[:, None, :], (B_QW, kh, kcw)).reshape(B_QW, kh * kcw)
        dcol = np.clip(kcol[None, :] - qcol[:, None], 1 - B_KW, B_KW - 1) + (B_KW - 1)
        bias = rpb[:, drow[:, None, :, None], dcol[None, :, None, :]]
        bias = jnp.transpose(bias, (1, 0, 2, 3, 4)).reshape(rows, B_HEADS, B_QW, kh * kcw)
        kb = kg[:, :, cs:cs + kcw][:, row_idx].reshape(bsz, rows, kh * kcw, B_HEADS, HEAD_DIM)
        vb = vg[:, :, cs:cs + kcw][:, row_idx].reshape(bsz, rows, kh * kcw, B_HEADS, HEAD_DIM)
        s = jnp.einsum('brqhd,brjhd->brhqj', qg[:, :, c0:c0 + B_QW], kb).astype(jnp.float32)
        s = jnp.where(mask, s + bias.astype(jnp.float32), NEG_INF)
        p = jax.nn.softmax(s, axis=-1).astype(v.dtype)
        outs.append(jnp.einsum('brhqj,brjhd->brqhd', p, vb))
    o = jnp.concatenate(outs, axis=2)
    return o.reshape(bsz, seq, B_HEADS * HEAD_DIM)


def diff_lambda_init(layer):
    return 0.8 - 0.6 * math.exp(-0.3 * layer)


def diff_attention(q, k, v, lam, norm_g, lam_init, cos, sin):
    bsz, seq = q.shape[:2]
    nb = seq // C_BLOCK
    q = apply_rope(q, cos, sin) * (HEAD_DIM ** -0.5)
    k = apply_rope(k, cos, sin)
    lf = lam.astype(jnp.float32)
    lam_val = jnp.exp(jnp.sum(lf[0] * lf[1])) - jnp.exp(jnp.sum(lf[2] * lf[3])) + lam_init
    qb = jnp.transpose(q.reshape(bsz, nb, C_BLOCK, C_HEADS, 2, HEAD_DIM), (1, 0, 2, 3, 4, 5))

    def block(qblk):
        s = jnp.einsum('bqhmd,bkhmd->bhmqk', qblk, k).astype(jnp.float32)
        p = jax.nn.softmax(s, axis=-1)
        a = (p[:, :, 0] - lam_val * p[:, :, 1]).astype(v.dtype)
        return jnp.einsum('bhqk,bkhe->bqhe', a, v)

    o = lax.map(block, qb)
    o = jnp.transpose(o, (1, 0, 2, 3, 4)).reshape(bsz, seq, C_HEADS, C_VDIM)
    of = o.astype(jnp.float32)
    of = of * lax.rsqrt(jnp.mean(jnp.square(of), axis=-1, keepdims=True) + LN_EPS)
    of = of.reshape(bsz, seq, C_HEADS * C_VDIM) * norm_g.astype(jnp.float32) * (1.0 - lam_init)
    return of.astype(v.dtype)


def _lin_combine(e1, e2):
    a1, b1 = e1
    a2, b2 = e2
    return a1 * a2, a2 * b1 + b2


def rg_lru_branch(xd, gate_in, conv_w, conv_b, wa, ba, wx, bx, lam):
    bsz, seq, width = xd.shape
    left = D_CONV // 2
    xc = lax.conv_general_dilated(xd, conv_w[:, None, :], window_strides=(1,),
                                  padding=((left, D_CONV - 1 - left),),
                                  dimension_numbers=('NWC', 'WIO', 'NWC'),
                                  feature_group_count=width) + conv_b
    xb = xc.reshape(bsz, seq, D_BLOCKS, width // D_BLOCKS)

    def direction(d, reverse):
        r = jax.nn.sigmoid(jnp.einsum('bsgi,gij->bsgj', xb, wa[d]).reshape(bsz, seq, width) + ba[d])
        i = jax.nn.sigmoid(jnp.einsum('bsgi,gij->bsgj', xb, wx[d]).reshape(bsz, seq, width) + bx[d])
        log_a = -LRU_C * r.astype(jnp.float32) * jax.nn.softplus(-lam[d].astype(jnp.float32))
        a = jnp.exp(log_a)
        b = jnp.sqrt(-jnp.expm1(2.0 * log_a)) * (i * xc).astype(jnp.float32)
        return lax.associative_scan(_lin_combine, (a, b), axis=1, reverse=reverse)[1]

    h = direction(0, False) + direction(1, True)
    return h.astype(xd.dtype) * jax.nn.gelu(gate_in)


def peer_ffn(h, wq, subkeys, u_tab, v_tab):
    bsz, seq, dm = h.shape
    t = bsz * seq
    ht = h.reshape(t, dm)
    q = (ht @ wq).reshape(t, PEER_HEADS, 2, PEER_QDIM // 2)
    s = jnp.einsum('thpd,hpkd->thpk', q, subkeys).astype(jnp.float32)
    sv, si = lax.top_k(s, PEER_TOPK)
    cand_s = (sv[:, :, 0, :, None] + sv[:, :, 1, None, :]).reshape(t, PEER_HEADS, PEER_TOPK * PEER_TOPK)
    cand_i = (si[:, :, 0, :, None] * PEER_KEYS + si[:, :, 1, None, :]).reshape(t, PEER_HEADS, PEER_TOPK * PEER_TOPK)
    top_s, top_j = lax.top_k(cand_s, PEER_TOPK)
    idx = jnp.take_along_axis(cand_i, top_j, axis=-1)
    g = jax.nn.softmax(top_s, axis=-1)
    nchunk = t // PEER_CHUNK

    def chunk(args):
        hc, ic, gc = args
        act = jax.nn.gelu(jnp.einsum('cd,chkd->chk', hc, u_tab[ic]))
        w = (gc * act.astype(jnp.float32)).astype(hc.dtype)
        return jnp.einsum('chk,chkd->cd', w, v_tab[ic])

    out = lax.map(chunk, (ht.reshape(nchunk, PEER_CHUNK, dm),
                          idx.reshape(nchunk, PEER_CHUNK, PEER_HEADS, PEER_TOPK),
                          g.reshape(nchunk, PEER_CHUNK, PEER_HEADS, PEER_TOPK)))
    return out.reshape(bsz, seq, dm)


def setup_inputs(seed: int = 0) -> dict:
    key = jax.random.key(seed)
    ks = jax.random.split(key, 26)
    f32 = jnp.float32

    def nrm(k, shape, scale):
        return jax.random.normal(k, shape, f32) * scale

    bs = D_WIDTH // D_BLOCKS
    u = jax.random.uniform(ks[18], (DEPTH, 2, D_WIDTH), f32, 0.9, 0.999)
    a0 = u ** (1.0 / LRU_C)
    return {
        'x': nrm(ks[0], (BATCH, SEQ, D_MODEL), 1.0),
        'c': nrm(ks[1], (BATCH, D_MODEL), 1.0),
        'w_ada': nrm(ks[2], (DEPTH, D_MODEL, 6 * D_MODEL), 0.5 * D_MODEL ** -0.5),
        'b_ada': nrm(ks[3], (DEPTH, 6 * D_MODEL), 0.02),
        'w_in': nrm(ks[4], (DEPTH, D_MODEL, IN_COLS), D_MODEL ** -0.5),
        'b_gate': nrm(ks[5], (DEPTH, N_BRANCH * D_MODEL), 0.02),
        'a_sink': nrm(ks[6], (DEPTH, A_HEADS), 0.5),
        'b_rpb': nrm(ks[7], (DEPTH, B_HEADS, 2 * B_KH - 1, 2 * B_KW - 1), 0.1),
        'c_lambda': nrm(ks[8], (DEPTH, 4, HEAD_DIM), 0.1),
        'c_norm_g': 1.0 + nrm(ks[9], (DEPTH, C_HEADS * C_VDIM), 0.02),
        'd_conv_w': nrm(ks[10], (DEPTH, D_CONV, D_WIDTH), D_CONV ** -0.5),
        'd_conv_b': nrm(ks[11], (DEPTH, D_WIDTH), 0.02),
        'd_wa': nrm(ks[12], (DEPTH, 2, D_BLOCKS, bs, bs), bs ** -0.5),
        'd_ba': nrm(ks[13], (DEPTH, 2, D_WIDTH), 0.02),
        'd_wx': nrm(ks[14], (DEPTH, 2, D_BLOCKS, bs, bs), bs ** -0.5),
        'd_bx': nrm(ks[15], (DEPTH, 2, D_WIDTH), 0.02),
        'd_lam': jnp.log(a0) - jnp.log1p(-a0),
        'w_branch': nrm(ks[16], (DEPTH, N_BRANCH, BRANCH_W, D_MODEL), BRANCH_W ** -0.5),
        'w_out': nrm(ks[17], (DEPTH, D_MODEL, D_MODEL), BETA * D_MODEL ** -0.5),
        'ln_g': 1.0 + nrm(ks[19], (DEPTH, 2, D_MODEL), 0.02),
        'ln_b': nrm(ks[20], (DEPTH, 2, D_MODEL), 0.02),
        'p_wq': nrm(ks[21], (DEPTH, D_MODEL, PEER_HEADS * PEER_QDIM), D_MODEL ** -0.5),
        'p_subkeys': nrm(ks[22], (DEPTH, PEER_HEADS, 2, PEER_KEYS, PEER_QDIM // 2), (PEER_QDIM // 2) ** -0.5),
        'p_u': nrm(ks[23], (DEPTH, PEER_N, D_MODEL), D_MODEL ** -0.5),
        'p_v': nrm(ks[24], (DEPTH, PEER_N, D_MODEL), BETA),
    }


def reference(x, c, w_ada, b_ada, w_in, b_gate, a_sink, b_rpb, c_lambda, c_norm_g,
              d_conv_w, d_conv_b, d_wa, d_ba, d_wx, d_bx, d_lam, w_branch, w_out,
              ln_g, ln_b, p_wq, p_subkeys, p_u, p_v):
    bsz, seq, _ = x.shape
    cos, sin = rope_tables(seq)
    c_act = jax.nn.silu(c)
    split_points = np.cumsum(IN_SPLITS)[:-1].tolist()
    for l in range(DEPTH):
        mod = (c_act @ w_ada[l] + b_ada[l]).reshape(bsz, 6, 1, D_MODEL)
        shift1, scale1, gate1 = mod[:, 0], mod[:, 1], mod[:, 2]
        shift2, scale2, gate2 = mod[:, 3], mod[:, 4], mod[:, 5]
        h = x * (1.0 + scale1) + shift1
        aq, ak, av, bq, bk, bv, cq, ck, cv, dx, dg, gl = jnp.split(h @ w_in[l], split_points, axis=-1)
        oa = window_gqa(aq.reshape(bsz, seq, A_HEADS, HEAD_DIM),
                        ak.reshape(bsz, seq, A_KV_HEADS, HEAD_DIM),
                        av.reshape(bsz, seq, A_KV_HEADS, HEAD_DIM), a_sink[l], cos, sin)
        ob = neighbourhood_attention(bq.reshape(bsz, seq, B_HEADS, HEAD_DIM),
                                     bk.reshape(bsz, seq, B_HEADS, HEAD_DIM),
                                     bv.reshape(bsz, seq, B_HEADS, HEAD_DIM), b_rpb[l])
        oc = diff_attention(cq.reshape(bsz, seq, C_HEADS, 2, HEAD_DIM),
                            ck.reshape(bsz, seq, C_HEADS, 2, HEAD_DIM),
                            cv.reshape(bsz, seq, C_HEADS, C_VDIM),
                            c_lambda[l], c_norm_g[l], diff_lambda_init(l), cos, sin)
        od = rg_lru_branch(dx, dg, d_conv_w[l], d_conv_b[l], d_wa[l], d_ba[l], d_wx[l], d_bx[l], d_lam[l])
        branches = jnp.stack([oa, ob, oc, od], axis=2)
        proj = jnp.einsum('bsne,ned->bsnd', branches, w_branch[l])
        gates = jax.nn.sigmoid(gl.reshape(bsz, seq, N_BRANCH, D_MODEL) + b_gate[l].reshape(N_BRANCH, D_MODEL))
        mix = jnp.sum(gates * proj, axis=2) @ w_out[l]
        x = layer_norm(ALPHA * x + gate1 * mix, ln_g[l, 0], ln_b[l, 0])
        h2 = x * (1.0 + scale2) + shift2
        y = peer_ffn(h2, p_wq[l], p_subkeys[l], p_u[l], p_v[l])
        x = layer_norm(ALPHA * x + gate2 * y, ln_g[l, 1], ln_b[l, 1])
    return x
```

```python
import functools
import math

import numpy as np
import jax
import jax.numpy as jnp
from jax import lax
from jax.experimental import pallas as pl
from jax.experimental.pallas import tpu as pltpu

F32 = jnp.float32
BF16 = jnp.bfloat16

LANES = 128
SUBLANES = 8
VMEM_LIMIT = 56 * 1024 * 1024

D_MODEL = 1024
DEPTH = 4
GRID_W = 64
HEAD_DIM = 64
ROPE_THETA = 10000.0
NEG_INF = -1e30
LN_EPS = 1e-5
A_HEADS = 8
A_KV_HEADS = 2
A_WINDOW = 128
A_BLOCK = 128
B_HEADS = 8
B_KH = 8
B_KW = 16
C_HEADS = 4
C_VDIM = 2 * HEAD_DIM
D_WIDTH = 512
D_BLOCKS = 8
D_CONV = 4
LRU_C = 8.0
N_BRANCH = 4
BRANCH_W = 512
PEER_HEADS = 8
PEER_KEYS = 128
PEER_N = PEER_KEYS * PEER_KEYS
PEER_QDIM = 256
PEER_TOPK = 16
ALPHA = (2.0 * DEPTH) ** 0.25

A_Q = A_HEADS * HEAD_DIM
A_KV = A_KV_HEADS * HEAD_DIM
B_QKV = B_HEADS * HEAD_DIM
C_QK = C_HEADS * 2 * HEAD_DIM
C_V = C_HEADS * C_VDIM
OFF_A = 0
OFF_B = A_Q + 2 * A_KV
OFF_C = OFF_B + 3 * B_QKV
OFF_D = OFF_C + 2 * C_QK + C_V
OFF_G = OFF_D + 2 * D_WIDTH

A_HEAD_ORDER = (0, 4, 1, 5, 2, 6, 3, 7)

B_QROWS = 4
B_KROWS = B_QROWS + B_KH


def _params(sem):
    return pltpu.CompilerParams(dimension_semantics=sem, vmem_limit_bytes=VMEM_LIMIT)


def _gelu(x):
    return x * (0.5 * (1.0 + jnp.tanh(math.sqrt(2.0 / math.pi) * (x + 0.044715 * (x * x * x)))))


def _layer_norm(y, g, b):
    mu = jnp.mean(y, axis=-1, keepdims=True)
    yc = y - mu
    var = jnp.mean(yc * yc, axis=-1, keepdims=True)
    return yc * lax.rsqrt(var + LN_EPS) * g + b


def _dot_nt(a, b):
    return lax.dot_general(a, b, (((1,), (1,)), ((), ())), preferred_element_type=F32)


def _low_half_mask():
    return lax.broadcasted_iota(jnp.int32, (1, LANES), 1) < HEAD_DIM


def _ada_kernel(c_ref, w_ref, b_ref, o_ref):
    c = c_ref[...]
    ca = c * jax.nn.sigmoid(c)
    o_ref[0] = jnp.dot(ca, w_ref[0], precision=lax.Precision.HIGHEST,
                       preferred_element_type=F32) + b_ref[0]


def _ada_call(c_pad, w_ada, b_ada):
    depth, d, cols = w_ada.shape
    rows = c_pad.shape[0]
    tn = 1024
    return pl.pallas_call(
        _ada_kernel,
        grid=(depth, cols // tn),
        in_specs=[pl.BlockSpec((rows, d), lambda l, j: (0, 0)),
                  pl.BlockSpec((1, d, tn), lambda l, j: (l, 0, j)),
                  pl.BlockSpec((1, 1, tn), lambda l, j: (l, 0, j))],
        out_specs=pl.BlockSpec((1, rows, tn), lambda l, j: (l, 0, j)),
        out_shape=jax.ShapeDtypeStruct((depth, rows, cols), F32),
        compiler_params=_params(("arbitrary", "arbitrary")),
        name="adaln_mod",
    )(c_pad, w_ada, b_ada.reshape(depth, 1, cols))


def _inproj_kernel(x_ref, sh_ref, sc_ref, wa_ref, wb_ref, wc_ref, wd_ref,
                   cos_ref, slo_ref, shi_ref,
                   aq_ref, akv_ref, bq_ref, bk_ref, bv_ref, cq_ref, ck_ref, cv_ref,
                   dx_ref, dg_ref):
    h = (x_ref[0] * (1.0 + sc_ref[0]) + sh_ref[0]).astype(BF16)
    cos = cos_ref[...]
    slo = slo_ref[...]
    shi = shi_ref[...]
    qscale = HEAD_DIM ** -0.5

    def rope(v):
        return (v * cos + pltpu.roll(v, LANES - HEAD_DIM // 2, 1) * slo
                + pltpu.roll(v, HEAD_DIM // 2, 1) * shi)

    def tiles(y, start, n):
        return [y[:, (start + i) * LANES:(start + i + 1) * LANES] for i in range(n)]

    ya = jnp.dot(h, wa_ref[...], preferred_element_type=F32)
    for i, t in enumerate(tiles(ya, 0, A_Q // LANES)):
        aq_ref[0, :, i * LANES:(i + 1) * LANES] = (rope(t) * qscale).astype(BF16)
    akv_ref[0, :, 0:LANES] = rope(ya[:, A_Q:A_Q + LANES]).astype(BF16)
    akv_ref[0, :, LANES:2 * LANES] = ya[:, A_Q + LANES:A_Q + 2 * LANES].astype(BF16)

    yb = jnp.dot(h, wb_ref[...], preferred_element_type=F32)
    bq_ref[0] = (yb[:, 0:B_QKV] * qscale).astype(BF16)
    bk_ref[0] = yb[:, B_QKV:2 * B_QKV].astype(BF16)
    bv_ref[0] = yb[:, 2 * B_QKV:3 * B_QKV].astype(BF16)

    yc = jnp.dot(h, wc_ref[...], preferred_element_type=F32)
    nt = C_QK // LANES
    for i, t in enumerate(tiles(yc, 0, nt)):
        cq_ref[0, :, i * LANES:(i + 1) * LANES] = (rope(t) * qscale).astype(BF16)
    for i, t in enumerate(tiles(yc, nt, nt)):
        ck_ref[0, :, i * LANES:(i + 1) * LANES] = rope(t).astype(BF16)
    cv_ref[0] = yc[:, 2 * C_QK:2 * C_QK + C_V].astype(BF16)

    yd = jnp.dot(h, wd_ref[...], preferred_element_type=F32)
    dx_ref[0] = yd[:, 0:D_WIDTH]
    dg_ref[0] = yd[:, D_WIDTH:2 * D_WIDTH]


def _inproj_call(x, shift, scale, wa, wb, wc, wd, cos_t, sin_lo, sin_hi, tm):
    bsz, seq, d = x.shape
    tok = lambda w: pl.BlockSpec((1, tm, w), lambda b, i: (b, i, 0))
    vec = pl.BlockSpec((1, 1, d), lambda b, i: (b, 0, 0))
    full = lambda w: pl.BlockSpec(w.shape, lambda b, i: (0, 0))
    tab = pl.BlockSpec((tm, LANES), lambda b, i: (i, 0))
    outs = [(A_Q, BF16), (2 * LANES, BF16), (B_QKV, BF16), (B_QKV, BF16), (B_QKV, BF16),
            (C_QK, BF16), (C_QK, BF16), (C_V, BF16), (D_WIDTH, F32), (D_WIDTH, F32)]
    return pl.pallas_call(
        _inproj_kernel,
        grid=(bsz, seq // tm),
        in_specs=[tok(d), vec, vec, full(wa), full(wb), full(wc), full(wd), tab, tab, tab],
        out_specs=[tok(w) for w, _ in outs],
        out_shape=[jax.ShapeDtypeStruct((bsz, seq, w), dt) for w, dt in outs],
        compiler_params=_params(("parallel", "arbitrary")),
        name="in_proj",
    )(x, shift, scale, wa, wb, wc, wd, cos_t, sin_lo, sin_hi)


def _window_kernel(sink_ref, q_ref, kvp_ref, kvc_ref, kvn_ref, o_ref, *, seq):
    n = pl.program_id(1)
    blk = A_BLOCK
    kv = jnp.concatenate([kvp_ref[0], kvc_ref[0], kvn_ref[0]], axis=0)
    k = kv[:, 0:LANES]
    v = kv[:, LANES:2 * LANES]
    qi = lax.broadcasted_iota(jnp.int32, (2 * blk, 3 * blk), 0) % blk
    kj = lax.broadcasted_iota(jnp.int32, (2 * blk, 3 * blk), 1)
    rel = kj - blk - qi
    kpos = n * blk + kj - blk
    valid = (jnp.abs(rel) <= A_WINDOW) & (kpos >= 0) & (kpos < seq)
    low_rows = lax.broadcasted_iota(jnp.int32, (2 * blk, 1), 0) < blk
    low = _low_half_mask()
    for pair in range(A_HEADS // 2):
        qp = q_ref[0, :, pair * LANES:(pair + 1) * LANES]
        zero = jnp.zeros_like(qp)
        q2 = jnp.concatenate([jnp.where(low, qp, zero), jnp.where(low, zero, qp)], axis=0)
        s = jnp.where(valid, _dot_nt(q2, k), NEG_INF)
        sk = jnp.where(low_rows, sink_ref[A_HEAD_ORDER[2 * pair]],
                       sink_ref[A_HEAD_ORDER[2 * pair + 1]])
        m = jnp.maximum(jnp.max(s, axis=-1, keepdims=True), sk)
        p = jnp.exp(s - m)
        p = p / (jnp.sum(p, axis=-1, keepdims=True) + jnp.exp(sk - m))
        o2 = jnp.dot(p.astype(BF16), v, preferred_element_type=F32)
        o_ref[0, :, pair * LANES:(pair + 1) * LANES] = jnp.where(low, o2[0:blk], o2[blk:]).astype(BF16)


def _window_call(sink, aq, akv):
    bsz, seq, _ = aq.shape
    nb = seq // A_BLOCK
    kvspec = lambda f: pl.BlockSpec((1, A_BLOCK, 2 * LANES), f)
    return pl.pallas_call(
        functools.partial(_window_kernel, seq=seq),
        grid=(bsz, nb),
        in_specs=[pl.BlockSpec(memory_space=pltpu.SMEM),
                  pl.BlockSpec((1, A_BLOCK, A_Q), lambda b, n: (b, n, 0)),
                  kvspec(lambda b, n: (b, jnp.maximum(n - 1, 0), 0)),
                  kvspec(lambda b, n: (b, n, 0)),
                  kvspec(lambda b, n: (b, jnp.minimum(n + 1, nb - 1), 0))],
        out_specs=pl.BlockSpec((1, A_BLOCK, A_Q), lambda b, n: (b, n, 0)),
        out_shape=jax.ShapeDtypeStruct((bsz, seq, A_Q), BF16),
        compiler_params=_params(("parallel", "arbitrary")),
        name="window_gqa",
    )(sink, aq, akv, akv, akv)


def _nbr_bias_tables(rpb, rows):
    groups = rows // B_QROWS
    qr = np.arange(B_QROWS)
    kr = np.arange(B_KROWS)
    row_onehot = np.zeros((3, B_QROWS, B_KROWS, 2 * B_KH - 1), np.float32)
    row_valid = np.zeros((3, B_QROWS, B_KROWS), bool)
    for var, g in enumerate((0, 1, groups - 1)):
        base = int(np.clip(g - 1, 0, groups - 3)) * B_QROWS
        r = g * B_QROWS + qr
        rstart = np.clip(r - B_KH // 2, 0, rows - B_KH)
        krow = base + kr
        ok = (krow[None, :] >= rstart[:, None]) & (krow[None, :] < rstart[:, None] + B_KH)
        drow = np.clip(krow[None, :] - r[:, None] + (B_KH - 1), 0, 2 * B_KH - 2)
        row_valid[var] = ok
        row_onehot[var, qr[:, None], kr[None, :], drow] = 1.0
    qc = np.arange(GRID_W)
    kc = np.arange(GRID_W)
    cstart = np.clip(qc - B_KW // 2, 0, GRID_W - B_KW)
    col_valid = (kc[None, :] >= cstart[:, None]) & (kc[None, :] < cstart[:, None] + B_KW)
    dcol = np.clip(kc[None, :] - qc[:, None], 1 - B_KW, B_KW - 1) + (B_KW - 1)
    col_onehot = np.zeros((GRID_W, GRID_W, 2 * B_KW - 1), np.float32)
    col_onehot[qc[:, None], kc[None, :], dcol] = 1.0
    valid = row_valid[:, :, None, :, None] & col_valid[None, None, :, None, :]
    hi = lax.Precision.HIGHEST
    by_col = jnp.einsum('lhab,qkb->lhaqk', rpb, jnp.asarray(col_onehot), precision=hi)
    tiles = jnp.einsum('lhaqk,vrsa->lvhrqsk', by_col, jnp.asarray(row_onehot), precision=hi)
    tiles = jnp.where(jnp.asarray(valid)[None, :, None], tiles, NEG_INF)
    depth = rpb.shape[0]
    return tiles.reshape(depth, 3, B_HEADS, B_QROWS * GRID_W, B_KROWS * GRID_W)


def _nbr_kernel(q_ref, k0_ref, k1_ref, k2_ref, v0_ref, v1_ref, v2_ref, bias_ref, o_ref):
    nq = B_QROWS * GRID_W
    k = jnp.concatenate([k0_ref[0], k1_ref[0], k2_ref[0]], axis=0)
    v = jnp.concatenate([v0_ref[0], v1_ref[0], v2_ref[0]], axis=0)
    low = _low_half_mask()
    for pair in range(B_HEADS // 2):
        cols = slice(pair * LANES, (pair + 1) * LANES)
        qp = q_ref[0, :, cols]
        zero = jnp.zeros_like(qp)
        q2 = jnp.concatenate([jnp.where(low, qp, zero), jnp.where(low, zero, qp)], axis=0)
        bias = jnp.concatenate([bias_ref[0, 0, 2 * pair], bias_ref[0, 0, 2 * pair + 1]], axis=0)
        s = _dot_nt(q2, k[:, cols]) + bias
        m = jnp.max(s, axis=-1, keepdims=True)
        p = jnp.exp(s - m)
        p = p / jnp.sum(p, axis=-1, keepdims=True)
        o2 = jnp.dot(p.astype(BF16), v[:, cols], preferred_element_type=F32)
        o_ref[0, :, cols] = jnp.where(low, o2[0:nq], o2[nq:]).astype(BF16)


def _nbr_call(bq, bk, bv, bias_tiles, layer):
    bsz, seq, w = bq.shape
    nq = B_QROWS * GRID_W
    groups = seq // nq
    start = lambda g: jnp.clip(g - 1, 0, groups - 3)
    kvspec = lambda off: pl.BlockSpec((1, nq, w), lambda b, g: (b, start(g) + off, 0))
    variant = lambda g: jnp.where(g == 0, 0, jnp.where(g == groups - 1, 2, 1))
    return pl.pallas_call(
        _nbr_kernel,
        grid=(bsz, groups),
        in_specs=[pl.BlockSpec((1, nq, w), lambda b, g: (b, g, 0)),
                  kvspec(0), kvspec(1), kvspec(2), kvspec(0), kvspec(1), kvspec(2),
                  pl.BlockSpec((1, 1, B_HEADS, nq, B_KROWS * GRID_W),
                               lambda b, g: (layer, variant(g), 0, 0, 0))],
        out_specs=pl.BlockSpec((1, nq, w), lambda b, g: (b, g, 0)),
        out_shape=jax.ShapeDtypeStruct((bsz, seq, w), BF16),
        compiler_params=_params(("parallel", "arbitrary")),
        name="neighbourhood_attn",
    )(bq, bk, bk, bk, bv, bv, bv, bias_tiles)


def _diff_kernel(lam_ref, ng_ref, q_ref, k_ref, v_ref, o_ref,
                 m1_ref, l1_ref, m2_ref, l2_ref, acc1_ref, acc2_ref, *, lam_init):
    j = pl.program_id(3)

    @pl.when(j == 0)
    def _():
        m1_ref[...] = jnp.full(m1_ref.shape, -jnp.inf, F32)
        m2_ref[...] = jnp.full(m2_ref.shape, -jnp.inf, F32)
        l1_ref[...] = jnp.zeros(l1_ref.shape, F32)
        l2_ref[...] = jnp.zeros(l2_ref.shape, F32)
        acc1_ref[...] = jnp.zeros(acc1_ref.shape, F32)
        acc2_ref[...] = jnp.zeros(acc2_ref.shape, F32)

    low = _low_half_mask()
    q = q_ref[0]
    k = k_ref[0]
    v = v_ref[0]
    zero = jnp.zeros_like(q)

    def update(qm, m_ref, l_ref, acc_ref):
        s = _dot_nt(qm, k)
        m_old = m_ref[...]
        m_new = jnp.maximum(m_old, jnp.max(s, axis=-1, keepdims=True))
        alpha = jnp.exp(m_old - m_new)
        p = jnp.exp(s - m_new)
        l_ref[...] = alpha * l_ref[...] + jnp.sum(p, axis=-1, keepdims=True)
        acc_ref[...] = alpha * acc_ref[...] + jnp.dot(p.astype(BF16), v, preferred_element_type=F32)
        m_ref[...] = m_new

    update(jnp.where(low, q, zero), m1_ref, l1_ref, acc1_ref)
    update(jnp.where(low, zero, q), m2_ref, l2_ref, acc2_ref)

    @pl.when(j == pl.num_programs(3) - 1)
    def _():
        lf = lam_ref[...]
        lam = (jnp.exp(jnp.sum(lf[0:1] * lf[1:2], axis=-1, keepdims=True))
               - jnp.exp(jnp.sum(lf[2:3] * lf[3:4], axis=-1, keepdims=True)) + lam_init)
        o = acc1_ref[...] / l1_ref[...] - lam * (acc2_ref[...] / l2_ref[...])
        o = o * lax.rsqrt(jnp.mean(o * o, axis=-1, keepdims=True) + LN_EPS)
        o_ref[0] = (o * ng_ref[...] * (1.0 - lam_init)).astype(BF16)


def _diff_call(lam, norm_g, cq, ck, cv, lam_init, tq, tk):
    bsz, seq, _ = cq.shape
    stat = pltpu.VMEM((tq, 1), F32)
    acc = pltpu.VMEM((tq, C_VDIM), F32)
    return pl.pallas_call(
        functools.partial(_diff_kernel, lam_init=lam_init),
        grid=(bsz, C_HEADS, seq // tq, seq // tk),
        in_specs=[pl.BlockSpec((4, HEAD_DIM), lambda b, h, i, j: (0, 0)),
                  pl.BlockSpec((1, C_VDIM), lambda b, h, i, j: (0, h)),
                  pl.BlockSpec((1, tq, LANES), lambda b, h, i, j: (b, i, h)),
                  pl.BlockSpec((1, tk, LANES), lambda b, h, i, j: (b, j, h)),
                  pl.BlockSpec((1, tk, C_VDIM), lambda b, h, i, j: (b, j, h))],
        out_specs=pl.BlockSpec((1, tq, C_VDIM), lambda b, h, i, j: (b, i, h)),
        out_shape=jax.ShapeDtypeStruct((bsz, seq, C_V), BF16),
        scratch_shapes=[stat, stat, stat, stat, acc, acc],
        compiler_params=_params(("parallel", "parallel", "parallel", "arbitrary")),
        name="diff_attn",
    )(lam, norm_g.reshape(1, C_V), cq, ck, cv)


def _lru_kernel(xf_ref, xfp_ref, xfn_ref, xb_ref, xbp_ref, xbn_ref, cw_ref, cb_ref,
                w_ref, bias_ref, lam_ref, hf_ref, hb_ref, ext_ref, carry_ref, *, tc):
    j = pl.program_id(1)
    nc = pl.num_programs(1)
    halo = 8
    left = D_CONV // 2

    @pl.when(j == 0)
    def _():
        carry_ref[...] = jnp.zeros(carry_ref.shape, F32)

    z = -lam_ref[...]
    softplus = jnp.maximum(z, 0.0) + jnp.log1p(jnp.exp(-jnp.abs(z)))
    rows = lax.broadcasted_iota(jnp.int32, (tc, 1), 0)

    def gates(d, cur_ref, prev_ref, next_ref, chunk):
        ext_ref[0:halo] = jnp.where(chunk > 0, prev_ref[0], 0.0)
        ext_ref[halo:halo + tc] = cur_ref[0]
        ext_ref[halo + tc:2 * halo + tc] = jnp.where(chunk < nc - 1, next_ref[0], 0.0)
        xc = cb_ref[...]
        for t in range(D_CONV):
            xc = xc + cw_ref[t:t + 1] * ext_ref[pl.ds(halo - left + t, tc), :]
        pre = jnp.dot(xc.astype(BF16), w_ref[d], preferred_element_type=F32) + bias_ref[d:d + 1]
        r = jax.nn.sigmoid(pre[:, 0:D_WIDTH])
        i = jax.nn.sigmoid(pre[:, D_WIDTH:2 * D_WIDTH])
        log_a = (-LRU_C) * r * softplus[d:d + 1]
        a = jnp.exp(log_a)
        b = jnp.sqrt(-jnp.tanh(log_a) * (a * a + 1.0)) * (i * xc)
        return a, b

    a, b = gates(0, xf_ref, xfp_ref, xfn_ref, j)
    step = 1
    while step < tc:
        keep = rows >= step
        a_sh = jnp.where(keep, pltpu.roll(a, step, 0), 1.0)
        b_sh = jnp.where(keep, pltpu.roll(b, step, 0), 0.0)
        b = a * b_sh + b
        a = a * a_sh
        step *= 2
    h = b + a * carry_ref[0:1]
    hf_ref[0] = h
    carry_ref[0:1] = h[tc - 1:tc]

    a, b = gates(1, xb_ref, xbp_ref, xbn_ref, nc - 1 - j)
    step = 1
    while step < tc:
        keep = rows < tc - step
        a_sh = jnp.where(keep, pltpu.roll(a, tc - step, 0), 1.0)
        b_sh = jnp.where(keep, pltpu.roll(b, tc - step, 0), 0.0)
        b = a * b_sh + b
        a = a * a_sh
        step *= 2
    h = b + a * carry_ref[1:2]
    hb_ref[0] = h
    carry_ref[1:2] = h[0:1]


def _lru_call(dx, conv_w, conv_b, w_gates, b_gates, lam, tc):
    bsz, seq, w = dx.shape
    nc = seq // tc
    hb = tc // 8
    nh = seq // 8
    cur = lambda f: pl.BlockSpec((1, tc, w), f)
    halo = lambda f: pl.BlockSpec((1, 8, w), f)
    full = lambda a: pl.BlockSpec(a.shape, lambda b, j: (0,) * a.ndim)
    mirror = lambda j: nc - 1 - j
    return pl.pallas_call(
        functools.partial(_lru_kernel, tc=tc),
        grid=(bsz, nc),
        in_specs=[cur(lambda b, j: (b, j, 0)),
                  halo(lambda b, j: (b, jnp.maximum(j * hb - 1, 0), 0)),
                  halo(lambda b, j: (b, jnp.minimum((j + 1) * hb, nh - 1), 0)),
                  cur(lambda b, j: (b, mirror(j), 0)),
                  halo(lambda b, j: (b, jnp.maximum(mirror(j) * hb - 1, 0), 0)),
                  halo(lambda b, j: (b, jnp.minimum((mirror(j) + 1) * hb, nh - 1), 0)),
                  full(conv_w), full(conv_b), full(w_gates), full(b_gates), full(lam)],
        out_specs=[cur(lambda b, j: (b, j, 0)), cur(lambda b, j: (b, mirror(j), 0))],
        out_shape=[jax.ShapeDtypeStruct((bsz, seq, w), F32)] * 2,
        scratch_shapes=[pltpu.VMEM((tc + 16, w), F32), pltpu.VMEM((2, w), F32)],
        compiler_params=_params(("parallel", "arbitrary")),
        name="rg_lru",
    )(dx, dx, dx, dx, dx, dx, conv_w, conv_b, w_gates, b_gates, lam)


def _merge_kernel(x_ref, sh1_ref, sc1_ref, g1_ref, sh2_ref, sc2_ref,
                  oa_ref, ob_ref, oc_ref, hf_ref, hb_ref, dg_ref,
                  wg_ref, bg_ref, wbr_ref, wo_ref, lng_ref, lnb_ref,
                  x1_ref, h2_ref):
    x = x_ref[0]
    h = (x * (1.0 + sc1_ref[0]) + sh1_ref[0]).astype(BF16)
    od = ((hf_ref[0] + hb_ref[0]) * _gelu(dg_ref[0])).astype(BF16)
    branches = (oa_ref[0], ob_ref[0], oc_ref[0], od)
    mixed = None
    for n in range(N_BRANCH):
        cols = slice(n * D_MODEL, (n + 1) * D_MODEL)
        gate = jax.nn.sigmoid(jnp.dot(h, wg_ref[:, cols], preferred_element_type=F32) + bg_ref[:, cols])
        term = gate * jnp.dot(branches[n], wbr_ref[n], preferred_element_type=F32)
        mixed = term if mixed is None else mixed + term
    mix = jnp.dot(mixed.astype(BF16), wo_ref[...], preferred_element_type=F32)
    x1 = _layer_norm(ALPHA * x + g1_ref[0] * mix, lng_ref[...], lnb_ref[...])
    x1_ref[0] = x1
    h2_ref[0] = (x1 * (1.0 + sc2_ref[0]) + sh2_ref[0]).astype(BF16)


def _merge_call(x, mods, oa, ob, oc, hf, hb, dg, wg, bg, wbr, wo, lng, lnb, tm):
    bsz, seq, d = x.shape
    tok = lambda w: pl.BlockSpec((1, tm, w), lambda b, i: (b, i, 0))
    vec = pl.BlockSpec((1, 1, d), lambda b, i: (b, 0, 0))
    full = lambda a: pl.BlockSpec(a.shape, lambda b, i: (0,) * a.ndim)
    return pl.pallas_call(
        _merge_kernel,
        grid=(bsz, seq // tm),
        in_specs=[tok(d)] + [vec] * 5 + [tok(BRANCH_W)] * 6
                 + [full(wg), full(bg), full(wbr), full(wo), full(lng), full(lnb)],
        out_specs=[tok(d), tok(d)],
        out_shape=[jax.ShapeDtypeStruct((bsz, seq, d), F32),
                   jax.ShapeDtypeStruct((bsz, seq, d), BF16)],
        compiler_params=_params(("parallel", "arbitrary")),
        name="merge_out_ln",
    )(x, *mods, oa, ob, oc, hf, hb, dg, wg, bg, wbr, wo, lng, lnb)


def _top_values(work, count):
    vals = []
    for _ in range(count):
        mx = jnp.max(work, axis=0, keepdims=True)
        vals.append(mx)
        work = jnp.where(work == mx, -jnp.inf, work)
    return vals


def _peer_route_kernel(h2_ref, wq_ref, sk_ref, s1_ref, s2_ref, tau_ref, cc_ref):
    q = jnp.dot(h2_ref[...], wq_ref[...], preferred_element_type=F32).astype(BF16)
    k = PEER_TOPK
    for head in range(PEER_HEADS):
        top = []
        for part, s_ref in enumerate((s1_ref, s2_ref)):
            hp = 2 * head + part
            st = _dot_nt(sk_ref[hp], q[:, hp * LANES:(hp + 1) * LANES])
            s_ref[head] = st
            top.append(jnp.concatenate(_top_values(st, k), axis=0))
        sv1, sv2 = top
        cand = [sv1[0:1] + sv2]
        cand += [sv1[a:a + 1] + sv2[0:8] for a in range(1, 8)]
        cand += [sv1[8:16] + sv2[0:1]]
        best = _top_values(jnp.concatenate(cand, axis=0), k)
        m = best[0]
        zsum = jnp.zeros_like(m)
        for t in best:
            zsum = zsum + jnp.exp(t - m)
        tau_ref[head:head + 1] = best[k - 1]
        cc_ref[head:head + 1] = m + jnp.log(zsum)


def _peer_route_call(h2, wq, subkeys, tm):
    tokens, d = h2.shape
    score_spec = pl.BlockSpec((PEER_HEADS, PEER_KEYS, tm), lambda i: (0, 0, i))
    stat_spec = pl.BlockSpec((PEER_HEADS, tm), lambda i: (0, i))
    score_shape = jax.ShapeDtypeStruct((PEER_HEADS, PEER_KEYS, tokens), F32)
    stat_shape = jax.ShapeDtypeStruct((PEER_HEADS, tokens), F32)
    return pl.pallas_call(
        _peer_route_kernel,
        grid=(tokens // tm,),
        in_specs=[pl.BlockSpec((tm, d), lambda i: (i, 0)),
                  pl.BlockSpec(wq.shape, lambda i: (0, 0)),
                  pl.BlockSpec(subkeys.shape, lambda i: (0, 0, 0))],
        out_specs=[score_spec, score_spec, stat_spec, stat_spec],
        out_shape=[score_shape, score_shape, stat_shape, stat_shape],
        compiler_params=_params(("parallel",)),
        name="peer_route",
    )(h2, wq, subkeys)


def _peer_mix_kernel(h2_ref, u_ref, vt_ref, s1_ref, s2_ref, tau_ref, cc_ref, x1_ref, g2_ref,
                     lng_ref, lnb_ref, o_ref, acc_ref, st_ref, w_ref, *, tn, tt):
    j = pl.program_id(1)

    @pl.when(j == 0)
    def _():
        acc_ref[...] = jnp.zeros(acc_ref.shape, F32)

    st_ref[...] = _dot_nt(u_ref[...], h2_ref[...])
    per_tile = tn // PEER_KEYS
    groups = SUBLANES // per_tile
    sub = j % groups

    def chunk(c, carry):
        lanes = pl.ds(pl.multiple_of(c * LANES, LANES), LANES)
        for a in range(per_tile):
            g = jnp.zeros((PEER_KEYS, LANES), F32)
            for head in range(PEER_HEADS):
                s1row = s1_ref[head, pl.ds(a, 1), lanes]
                for alt in range(1, groups):
                    s1row = jnp.where(sub == alt, s1_ref[head, pl.ds(alt * per_tile + a, 1), lanes], s1row)
                zsum = s1row + s2_ref[head, :, lanes]
                g = g + jnp.where(zsum >= tau_ref[head:head + 1, lanes],
                                  jnp.exp(zsum - cc_ref[head:head + 1, lanes]), 0.0)
            rows = pl.ds(a * PEER_KEYS, PEER_KEYS)
            w_ref[rows, lanes] = (_gelu(st_ref[rows, lanes]) * g).astype(BF16)
        return carry

    lax.fori_loop(0, tt // LANES, chunk, 0)
    acc_ref[...] += jnp.dot(vt_ref[...], w_ref[...], preferred_element_type=F32)

    @pl.when(j == pl.num_programs(1) - 1)
    def _():
        y = acc_ref[...].T
        o_ref[...] = _layer_norm(ALPHA * x1_ref[...] + g2_ref[0] * y, lng_ref[...], lnb_ref[...])


def _peer_mix_call(h2, u, vt, s1, s2, tau, cc, x1, gate2, lng, lnb, seq, tt, tn):
    tokens, d = h2.shape
    n = u.shape[0]
    per_batch = seq // tt
    per_tile = tn // PEER_KEYS
    assert tn % PEER_KEYS == 0 and SUBLANES % per_tile == 0
    return pl.pallas_call(
        functools.partial(_peer_mix_kernel, tn=tn, tt=tt),
        grid=(tokens // tt, n // tn),
        in_specs=[pl.BlockSpec((tt, d), lambda i, j: (i, 0)),
                  pl.BlockSpec((tn, d), lambda i, j: (j, 0)),
                  pl.BlockSpec((d, tn), lambda i, j: (0, j)),
                  pl.BlockSpec((PEER_HEADS, SUBLANES, tt), lambda i, j: (0, j * per_tile // SUBLANES, i)),
                  pl.BlockSpec((PEER_HEADS, PEER_KEYS, tt), lambda i, j: (0, 0, i)),
                  pl.BlockSpec((PEER_HEADS, tt), lambda i, j: (0, i)),
                  pl.BlockSpec((PEER_HEADS, tt), lambda i, j: (0, i)),
                  pl.BlockSpec((tt, d), lambda i, j: (i, 0)),
                  pl.BlockSpec((1, 1, d), lambda i, j: (i // per_batch, 0, 0)),
                  pl.BlockSpec((1, d), lambda i, j: (0, 0)),
                  pl.BlockSpec((1, d), lambda i, j: (0, 0))],
        out_specs=pl.BlockSpec((tt, d), lambda i, j: (i, 0)),
        out_shape=jax.ShapeDtypeStruct((tokens, d), F32),
        scratch_shapes=[pltpu.VMEM((d, tt), F32), pltpu.VMEM((tn, tt), F32),
                        pltpu.VMEM((tn, tt), BF16)],
        compiler_params=_params(("parallel", "arbitrary")),
        name="peer_mix",
    )(h2, u, vt, s1, s2, tau, cc, x1, gate2, lng, lnb)


def _rope_tables(seq):
    inv = 1.0 / (ROPE_THETA ** (jnp.arange(0, HEAD_DIM, 2, dtype=F32) / HEAD_DIM))
    ang = jnp.arange(seq, dtype=F32)[:, None] * inv[None, :]
    cos, sin = jnp.cos(ang), jnp.sin(ang)
    zero = jnp.zeros_like(sin)
    reps = LANES // HEAD_DIM
    cos_t = jnp.tile(jnp.concatenate([cos, cos], axis=1), (1, reps))
    sin_lo = jnp.tile(jnp.concatenate([-sin, zero], axis=1), (1, reps))
    sin_hi = jnp.tile(jnp.concatenate([zero, sin], axis=1), (1, reps))
    return cos_t, sin_lo, sin_hi


def _block_diag(w):
    nb, bs, _ = w.shape
    eye = jnp.eye(nb, dtype=w.dtype)
    return (eye[:, None, :, None] * w[:, :, None, :]).reshape(nb * bs, nb * bs)


def _tile(n, want):
    t = min(n, want)
    assert n % t == 0
    return t


def kernel(x, c, w_ada, b_ada, w_in, b_gate, a_sink, b_rpb, c_lambda, c_norm_g, d_conv_w, d_conv_b, d_wa, d_ba, d_wx, d_bx, d_lam, w_branch, w_out, ln_g, ln_b, p_wq, p_subkeys, p_u, p_v):
    bsz, seq, d = x.shape
    tokens = bsz * seq
    rows = seq // GRID_W
    assert d == D_MODEL and seq % (B_QROWS * GRID_W) == 0 and rows // B_QROWS >= 3

    c_pad = jnp.pad(c, ((0, 8 - bsz), (0, 0)))
    mod = _ada_call(c_pad, w_ada, b_ada)[:, :bsz].reshape(DEPTH, bsz, 6, 1, d)
    cos_t, sin_lo, sin_hi = _rope_tables(seq)
    bias_tiles = _nbr_bias_tables(b_rpb, rows)

    head_cols = np.concatenate([np.arange(h * HEAD_DIM, (h + 1) * HEAD_DIM) for h in A_HEAD_ORDER])
    w_in16 = w_in.astype(BF16)
    wg16 = w_in16[:, :, OFF_G:]
    wbr16 = w_branch.astype(BF16)
    wo16 = w_out.astype(BF16)
    wq16 = p_wq.astype(BF16)
    sk16 = p_subkeys.astype(BF16).reshape(DEPTH, 2 * PEER_HEADS, PEER_KEYS, PEER_QDIM // 2)
    u16 = p_u.astype(BF16)
    vt16 = jnp.swapaxes(p_v, 1, 2).astype(BF16)

    for l in range(DEPTH):
        mods = [mod[l, :, i] for i in range(6)]
        shift1, scale1, gate1, shift2, scale2, gate2 = mods
        wl = w_in16[l]
        wa = jnp.concatenate([wl[:, head_cols], wl[:, A_Q:OFF_B]], axis=1)
        aq, akv, bq, bk, bv, cq, ck, cv, dx, dg = _inproj_call(
            x, shift1, scale1, wa, wl[:, OFF_B:OFF_C], wl[:, OFF_C:OFF_D], wl[:, OFF_D:OFF_G],
            cos_t, sin_lo, sin_hi, _tile(seq, 512))

        oa = _window_call(a_sink[l], aq, akv)
        ob = _nbr_call(bq, bk, bv, bias_tiles, l)
        lam_init = 0.8 - 0.6 * math.exp(-0.3 * l)
        oc = _diff_call(c_lambda[l], c_norm_g[l], cq, ck, cv, lam_init,
                        _tile(seq, 512), _tile(seq, 1024))
        w_gates = jnp.stack([
            jnp.concatenate([_block_diag(d_wa[l, dr]), _block_diag(d_wx[l, dr])], axis=1)
            for dr in range(2)]).astype(BF16)
        b_gates = jnp.concatenate([d_ba[l], d_bx[l]], axis=1)
        hf, hb = _lru_call(dx, d_conv_w[l], d_conv_b[l].reshape(1, D_WIDTH), w_gates, b_gates,
                           d_lam[l], _tile(seq, 256))

        wbr = wbr16[l]
        wbr = wbr.at[0].set(wbr[0].reshape(A_HEADS, HEAD_DIM, d)[np.array(A_HEAD_ORDER)].reshape(BRANCH_W, d))
        x1, h2 = _merge_call(x, [shift1, scale1, gate1, shift2, scale2], oa, ob, oc, hf, hb, dg,
                             wg16[l], b_gate[l].reshape(1, -1), wbr, wo16[l],
                             ln_g[l, 0].reshape(1, d), ln_b[l, 0].reshape(1, d), _tile(seq, 256))

        h2f = h2.reshape(tokens, d)
        s1, s2, tau, cc = _peer_route_call(h2f, wq16[l], sk16[l], _tile(tokens, 512))
        xn = _peer_mix_call(h2f, u16[l], vt16[l], s1, s2, tau, cc, x1.reshape(tokens, d), gate2,
                            ln_g[l, 1].reshape(1, d), ln_b[l, 1].reshape(1, d),
                            seq, _tile(seq, 1024), 512)
        x = xn.reshape(bsz, seq, d)
    return x
```

```python
import functools
import math

import numpy as np
import jax
import jax.numpy as jnp
from jax import lax
from jax.experimental import pallas as pl
from jax.experimental.pallas import tpu as pltpu

F32 = jnp.float32
BF16 = jnp.bfloat16

LANES = 128
SUBLANES = 8
VMEM_LIMIT = 56 * 1024 * 1024

D_MODEL = 1024
DEPTH = 4
GRID_W = 64
HEAD_DIM = 64
ROPE_THETA = 10000.0
NEG_INF = -1e30
LN_EPS = 1e-5
A_HEADS = 8
A_KV_HEADS = 2
A_WINDOW = 128
A_BLOCK = 128
B_HEADS = 8
B_KH = 8
B_KW = 16
C_HEADS = 4
C_VDIM = 2 * HEAD_DIM
D_WIDTH = 512
D_BLOCKS = 8
D_CONV = 4
LRU_C = 8.0
N_BRANCH = 4
BRANCH_W = 512
PEER_HEADS = 8
PEER_KEYS = 128
PEER_N = PEER_KEYS * PEER_KEYS
PEER_QDIM = 256
PEER_TOPK = 16
ALPHA = (2.0 * DEPTH) ** 0.25
LOG2_E = math.log2(math.e)

A_Q = A_HEADS * HEAD_DIM
A_KV = A_KV_HEADS * HEAD_DIM
B_QKV = B_HEADS * HEAD_DIM
C_QK = C_HEADS * 2 * HEAD_DIM
C_V = C_HEADS * C_VDIM
OFF_A = 0
OFF_B = A_Q + 2 * A_KV
OFF_C = OFF_B + 3 * B_QKV
OFF_D = OFF_C + 2 * C_QK + C_V
OFF_G = OFF_D + 2 * D_WIDTH

A_HEAD_ORDER = (0, 4, 1, 5, 2, 6, 3, 7)

DIFF_TQ, DIFF_TK = 512, 2048

B_QROWS = 4
B_KROWS = B_QROWS + B_KH


def _params(sem):
    return pltpu.CompilerParams(dimension_semantics=sem, vmem_limit_bytes=VMEM_LIMIT)


def _gelu(x):
    return x * (0.5 * (1.0 + jnp.tanh(math.sqrt(2.0 / math.pi) * (x + 0.044715 * (x * x * x)))))


def _layer_norm(y, g, b):
    mu = jnp.mean(y, axis=-1, keepdims=True)
    yc = y - mu
    var = jnp.mean(yc * yc, axis=-1, keepdims=True)
    return yc * lax.rsqrt(var + LN_EPS) * g + b


def _dot_nt(a, b):
    return lax.dot_general(a, b, (((1,), (1,)), ((), ())), preferred_element_type=F32)


def _low_half_mask():
    return lax.broadcasted_iota(jnp.int32, (1, LANES), 1) < HEAD_DIM


def _ada_kernel(c_ref, w_ref, b_ref, o_ref):
    c = c_ref[...]
    ca = c * jax.nn.sigmoid(c)
    o_ref[0] = jnp.dot(ca, w_ref[0], precision=lax.Precision.HIGHEST,
                       preferred_element_type=F32) + b_ref[0]


def _ada_call(c_pad, w_ada, b_ada):
    depth, d, cols = w_ada.shape
    rows = c_pad.shape[0]
    tn = 1024
    return pl.pallas_call(
        _ada_kernel,
        grid=(depth, cols // tn),
        in_specs=[pl.BlockSpec((rows, d), lambda l, j: (0, 0)),
                  pl.BlockSpec((1, d, tn), lambda l, j: (l, 0, j)),
                  pl.BlockSpec((1, 1, tn), lambda l, j: (l, 0, j))],
        out_specs=pl.BlockSpec((1, rows, tn), lambda l, j: (l, 0, j)),
        out_shape=jax.ShapeDtypeStruct((depth, rows, cols), F32),
        compiler_params=_params(("arbitrary", "arbitrary")),
        name="adaln_mod",
    )(c_pad, w_ada, b_ada.reshape(depth, 1, cols))


def _inproj_kernel(x_ref, sh_ref, sc_ref, wa_ref, wb_ref, wc_ref, wd_ref,
                   cos_ref, slo_ref, shi_ref,
                   aq_ref, akv_ref, bq_ref, bk_ref, bv_ref, cq_ref, ck_ref, cv_ref,
                   dx_ref, dg_ref):
    h = (x_ref[0] * (1.0 + sc_ref[0]) + sh_ref[0]).astype(BF16)
    cos = cos_ref[...]
    slo = slo_ref[...]
    shi = shi_ref[...]
    qscale = HEAD_DIM ** -0.5

    def rope(v):
        return (v * cos + pltpu.roll(v, LANES - HEAD_DIM // 2, 1) * slo
                + pltpu.roll(v, HEAD_DIM // 2, 1) * shi)

    def tiles(y, start, n):
        return [y[:, (start + i) * LANES:(start + i + 1) * LANES] for i in range(n)]

    ya = jnp.dot(h, wa_ref[...], preferred_element_type=F32)
    for i, t in enumerate(tiles(ya, 0, A_Q // LANES)):
        aq_ref[0, :, i * LANES:(i + 1) * LANES] = (rope(t) * qscale).astype(BF16)
    akv_ref[0, :, 0:LANES] = rope(ya[:, A_Q:A_Q + LANES]).astype(BF16)
    akv_ref[0, :, LANES:2 * LANES] = ya[:, A_Q + LANES:A_Q + 2 * LANES].astype(BF16)

    yb = jnp.dot(h, wb_ref[...], preferred_element_type=F32)
    bq_ref[0] = (yb[:, 0:B_QKV] * qscale).astype(BF16)
    bk_ref[0] = yb[:, B_QKV:2 * B_QKV].astype(BF16)
    bv_ref[0] = yb[:, 2 * B_QKV:3 * B_QKV].astype(BF16)

    yc = jnp.dot(h, wc_ref[...], preferred_element_type=F32)
    nt = C_QK // LANES
    for i, t in enumerate(tiles(yc, 0, nt)):
        cq_ref[0, :, i * LANES:(i + 1) * LANES] = (rope(t) * qscale).astype(BF16)
    for i, t in enumerate(tiles(yc, nt, nt)):
        ck_ref[0, :, i * LANES:(i + 1) * LANES] = rope(t).astype(BF16)
    cv_ref[0] = yc[:, 2 * C_QK:2 * C_QK + C_V].astype(BF16)

    yd = jnp.dot(h, wd_ref[...], preferred_element_type=F32)
    dx_ref[0] = yd[:, 0:D_WIDTH]
    dg_ref[0] = yd[:, D_WIDTH:2 * D_WIDTH]


def _inproj_call(x, shift, scale, wa, wb, wc, wd, cos_t, sin_lo, sin_hi, tm):
    bsz, seq, d = x.shape
    tok = lambda w: pl.BlockSpec((1, tm, w), lambda b, i: (b, i, 0))
    vec = pl.BlockSpec((1, 1, d), lambda b, i: (b, 0, 0))
    full = lambda w: pl.BlockSpec(w.shape, lambda b, i: (0, 0))
    tab = pl.BlockSpec((tm, LANES), lambda b, i: (i, 0))
    outs = [(A_Q, BF16), (2 * LANES, BF16), (B_QKV, BF16), (B_QKV, BF16), (B_QKV, BF16),
            (C_QK, BF16), (C_QK, BF16), (C_V, BF16), (D_WIDTH, F32), (D_WIDTH, F32)]
    return pl.pallas_call(
        _inproj_kernel,
        grid=(bsz, seq // tm),
        in_specs=[tok(d), vec, vec, full(wa), full(wb), full(wc), full(wd), tab, tab, tab],
        out_specs=[tok(w) for w, _ in outs],
        out_shape=[jax.ShapeDtypeStruct((bsz, seq, w), dt) for w, dt in outs],
        compiler_params=_params(("parallel", "arbitrary")),
        name="in_proj",
    )(x, shift, scale, wa, wb, wc, wd, cos_t, sin_lo, sin_hi)


def _window_kernel(sink_ref, q_ref, kvp_ref, kvc_ref, kvn_ref, o_ref, *, seq):
    n = pl.program_id(1)
    blk = A_BLOCK
    kv = jnp.concatenate([kvp_ref[0], kvc_ref[0], kvn_ref[0]], axis=0)
    k = kv[:, 0:LANES]
    v = kv[:, LANES:2 * LANES]
    qi = lax.broadcasted_iota(jnp.int32, (2 * blk, 3 * blk), 0) % blk
    kj = lax.broadcasted_iota(jnp.int32, (2 * blk, 3 * blk), 1)
    rel = kj - blk - qi
    kpos = n * blk + kj - blk
    valid = (jnp.abs(rel) <= A_WINDOW) & (kpos >= 0) & (kpos < seq)
    low_rows = lax.broadcasted_iota(jnp.int32, (2 * blk, 1), 0) < blk
    low = _low_half_mask()
    for pair in range(A_HEADS // 2):
        qp = q_ref[0, :, pair * LANES:(pair + 1) * LANES]
        zero = jnp.zeros_like(qp)
        q2 = jnp.concatenate([jnp.where(low, qp, zero), jnp.where(low, zero, qp)], axis=0)
        s = jnp.where(valid, _dot_nt(q2, k), NEG_INF)
        sk = jnp.where(low_rows, sink_ref[A_HEAD_ORDER[2 * pair]],
                       sink_ref[A_HEAD_ORDER[2 * pair + 1]])
        m = jnp.maximum(jnp.max(s, axis=-1, keepdims=True), sk)
        p = jnp.exp(s - m)
        p = p / (jnp.sum(p, axis=-1, keepdims=True) + jnp.exp(sk - m))
        o2 = jnp.dot(p.astype(BF16), v, preferred_element_type=F32)
        o_ref[0, :, pair * LANES:(pair + 1) * LANES] = jnp.where(low, o2[0:blk], o2[blk:]).astype(BF16)


def _window_call(sink, aq, akv):
    bsz, seq, _ = aq.shape
    nb = seq // A_BLOCK
    kvspec = lambda f: pl.BlockSpec((1, A_BLOCK, 2 * LANES), f)
    return pl.pallas_call(
        functools.partial(_window_kernel, seq=seq),
        grid=(bsz, nb),
        in_specs=[pl.BlockSpec(memory_space=pltpu.SMEM),
                  pl.BlockSpec((1, A_BLOCK, A_Q), lambda b, n: (b, n, 0)),
                  kvspec(lambda b, n: (b, jnp.maximum(n - 1, 0), 0)),
                  kvspec(lambda b, n: (b, n, 0)),
                  kvspec(lambda b, n: (b, jnp.minimum(n + 1, nb - 1), 0))],
        out_specs=pl.BlockSpec((1, A_BLOCK, A_Q), lambda b, n: (b, n, 0)),
        out_shape=jax.ShapeDtypeStruct((bsz, seq, A_Q), BF16),
        compiler_params=_params(("parallel", "arbitrary")),
        name="window_gqa",
    )(sink, aq, akv, akv, akv)


def _nbr_bias_tables(rpb, rows):
    groups = rows // B_QROWS
    qr = np.arange(B_QROWS)
    kr = np.arange(B_KROWS)
    row_onehot = np.zeros((3, B_QROWS, B_KROWS, 2 * B_KH - 1), np.float32)
    row_valid = np.zeros((3, B_QROWS, B_KROWS), bool)
    for var, g in enumerate((0, 1, groups - 1)):
        base = int(np.clip(g - 1, 0, groups - 3)) * B_QROWS
        r = g * B_QROWS + qr
        rstart = np.clip(r - B_KH // 2, 0, rows - B_KH)
        krow = base + kr
        ok = (krow[None, :] >= rstart[:, None]) & (krow[None, :] < rstart[:, None] + B_KH)
        drow = np.clip(krow[None, :] - r[:, None] + (B_KH - 1), 0, 2 * B_KH - 2)
        row_valid[var] = ok
        row_onehot[var, qr[:, None], kr[None, :], drow] = 1.0
    qc = np.arange(GRID_W)
    kc = np.arange(GRID_W)
    cstart = np.clip(qc - B_KW // 2, 0, GRID_W - B_KW)
    col_valid = (kc[None, :] >= cstart[:, None]) & (kc[None, :] < cstart[:, None] + B_KW)
    dcol = np.clip(kc[None, :] - qc[:, None], 1 - B_KW, B_KW - 1) + (B_KW - 1)
    col_onehot = np.zeros((GRID_W, GRID_W, 2 * B_KW - 1), np.float32)
    col_onehot[qc[:, None], kc[None, :], dcol] = 1.0
    valid = row_valid[:, :, None, :, None] & col_valid[None, None, :, None, :]
    hi = lax.Precision.HIGHEST
    by_col = jnp.einsum('lhab,qkb->lhaqk', rpb, jnp.asarray(col_onehot), precision=hi)
    tiles = jnp.einsum('lhaqk,vrsa->lvhrqsk', by_col, jnp.asarray(row_onehot), precision=hi)
    tiles = jnp.where(jnp.asarray(valid)[None, :, None], tiles, NEG_INF)
    depth = rpb.shape[0]
    return tiles.reshape(depth, 3, B_HEADS, B_QROWS * GRID_W, B_KROWS * GRID_W)


def _nbr_kernel(q_ref, k0_ref, k1_ref, k2_ref, v0_ref, v1_ref, v2_ref, bias_ref, o_ref):
    nq = B_QROWS * GRID_W
    k = jnp.concatenate([k0_ref[0], k1_ref[0], k2_ref[0]], axis=0)
    v = jnp.concatenate([v0_ref[0], v1_ref[0], v2_ref[0]], axis=0)
    low = _low_half_mask()
    for pair in range(B_HEADS // 2):
        cols = slice(pair * LANES, (pair + 1) * LANES)
        qp = q_ref[0, :, cols]
        zero = jnp.zeros_like(qp)
        q2 = jnp.concatenate([jnp.where(low, qp, zero), jnp.where(low, zero, qp)], axis=0)
        bias = jnp.concatenate([bias_ref[0, 0, 2 * pair], bias_ref[0, 0, 2 * pair + 1]], axis=0)
        s = _dot_nt(q2, k[:, cols]) + bias
        m = jnp.max(s, axis=-1, keepdims=True)
        p = jnp.exp(s - m)
        p = p / jnp.sum(p, axis=-1, keepdims=True)
        o2 = jnp.dot(p.astype(BF16), v[:, cols], preferred_element_type=F32)
        o_ref[0, :, cols] = jnp.where(low, o2[0:nq], o2[nq:]).astype(BF16)


def _nbr_call(bq, bk, bv, bias_tiles, layer):
    bsz, seq, w = bq.shape
    nq = B_QROWS * GRID_W
    groups = seq // nq
    start = lambda g: jnp.clip(g - 1, 0, groups - 3)
    kvspec = lambda off: pl.BlockSpec((1, nq, w), lambda b, g: (b, start(g) + off, 0))
    variant = lambda g: jnp.where(g == 0, 0, jnp.where(g == groups - 1, 2, 1))
    return pl.pallas_call(
        _nbr_kernel,
        grid=(bsz, groups),
        in_specs=[pl.BlockSpec((1, nq, w), lambda b, g: (b, g, 0)),
                  kvspec(0), kvspec(1), kvspec(2), kvspec(0), kvspec(1), kvspec(2),
                  pl.BlockSpec((1, 1, B_HEADS, nq, B_KROWS * GRID_W),
                               lambda b, g: (layer, variant(g), 0, 0, 0))],
        out_specs=pl.BlockSpec((1, nq, w), lambda b, g: (b, g, 0)),
        out_shape=jax.ShapeDtypeStruct((bsz, seq, w), BF16),
        compiler_params=_params(("parallel", "arbitrary")),
        name="neighbourhood_attn",
    )(bq, bk, bk, bk, bv, bv, bv, bias_tiles)


def _diff_kernel(lam_ref, ng_ref, q_ref, k_ref, v_ref, o_ref, m_ref, acc_ref, *, lam_init):
    j = pl.program_id(3)
    tq = q_ref.shape[1]
    tk = k_ref.shape[1]

    @pl.when(j == 0)
    def _():
        m_ref[...] = jnp.full(m_ref.shape, -jnp.inf, F32)
        acc_ref[...] = jnp.zeros(acc_ref.shape, F32)

    low = _low_half_mask()
    q = q_ref[0]
    k = k_ref[0]
    zero = jnp.zeros_like(q)
    v_ext = jnp.concatenate([v_ref[0], jnp.ones((tk, LANES), BF16)], axis=1)
    scores = [_dot_nt(qm, k) for qm in (jnp.where(low, q, zero), jnp.where(low, zero, q))]
    for mp, s in enumerate(scores):
        tiles = [s[:, c * LANES:(c + 1) * LANES] for c in range(tk // LANES)]
        part = tiles[0]
        for t in tiles[1:]:
            part = jnp.maximum(part, t)
        m_old = m_ref[mp]
        m_new = jnp.maximum(m_old, jnp.broadcast_to(jnp.max(part, axis=-1, keepdims=True), (tq, LANES)))
        alpha = jnp.exp(m_old - m_new)
        p = jnp.concatenate([jnp.exp(t - m_new) for t in tiles], axis=1).astype(BF16)
        acc_ref[mp] = (jnp.concatenate([alpha, alpha], axis=1) * acc_ref[mp]
                       + jnp.dot(p, v_ext, preferred_element_type=F32))
        m_ref[mp] = m_new

    @pl.when(j == pl.num_programs(3) - 1)
    def _():
        lf = lam_ref[...]
        lam = (jnp.exp(jnp.sum(lf[0:1] * lf[1:2], axis=-1, keepdims=True))
               - jnp.exp(jnp.sum(lf[2:3] * lf[3:4], axis=-1, keepdims=True)) + lam_init)
        a1 = acc_ref[0]
        a2 = acc_ref[1]
        o = a1[:, 0:C_VDIM] / a1[:, C_VDIM:] - lam * (a2[:, 0:C_VDIM] / a2[:, C_VDIM:])
        o = o * lax.rsqrt(jnp.mean(o * o, axis=-1, keepdims=True) + LN_EPS)
        o_ref[0] = (o * ng_ref[...] * (1.0 - lam_init)).astype(BF16)


def _diff_call(lam, norm_g, cq, ck, cv, lam_init, tq, tk):
    bsz, seq, _ = cq.shape
    return pl.pallas_call(
        functools.partial(_diff_kernel, lam_init=lam_init),
        grid=(bsz, C_HEADS, seq // tq, seq // tk),
        in_specs=[pl.BlockSpec((4, HEAD_DIM), lambda b, h, i, j: (0, 0)),
                  pl.BlockSpec((1, C_VDIM), lambda b, h, i, j: (0, h)),
                  pl.BlockSpec((1, tq, LANES), lambda b, h, i, j: (b, i, h)),
                  pl.BlockSpec((1, tk, LANES), lambda b, h, i, j: (b, j, h)),
                  pl.BlockSpec((1, tk, C_VDIM), lambda b, h, i, j: (b, j, h))],
        out_specs=pl.BlockSpec((1, tq, C_VDIM), lambda b, h, i, j: (b, i, h)),
        out_shape=jax.ShapeDtypeStruct((bsz, seq, C_V), BF16),
        scratch_shapes=[pltpu.VMEM((2, tq, LANES), F32), pltpu.VMEM((2, tq, C_VDIM + LANES), F32)],
        compiler_params=_params(("parallel", "parallel", "parallel", "arbitrary")),
        name="diff_attn",
    )(lam, norm_g.reshape(1, C_V), cq, ck, cv)


def _lru_kernel(xf_ref, xfp_ref, xfn_ref, xb_ref, xbp_ref, xbn_ref, cw_ref, cb_ref,
                w_ref, bias_ref, lam_ref, hf_ref, hb_ref, ext_ref, carry_ref, *, tc):
    j = pl.program_id(1)
    nc = pl.num_programs(1)
    halo = 8
    left = D_CONV // 2

    @pl.when(j == 0)
    def _():
        carry_ref[...] = jnp.zeros(carry_ref.shape, F32)

    z = -lam_ref[...]
    softplus = jnp.maximum(z, 0.0) + jnp.log1p(jnp.exp(-jnp.abs(z)))
    rows = lax.broadcasted_iota(jnp.int32, (tc, 1), 0)

    def gates(d, cur_ref, prev_ref, next_ref, chunk):
        ext_ref[0:halo] = jnp.where(chunk > 0, prev_ref[0], 0.0)
        ext_ref[halo:halo + tc] = cur_ref[0]
        ext_ref[halo + tc:2 * halo + tc] = jnp.where(chunk < nc - 1, next_ref[0], 0.0)
        xc = cb_ref[...]
        for t in range(D_CONV):
            xc = xc + cw_ref[t:t + 1] * ext_ref[pl.ds(halo - left + t, tc), :]
        pre = jnp.dot(xc.astype(BF16), w_ref[d], preferred_element_type=F32) + bias_ref[d:d + 1]
        r = jax.nn.sigmoid(pre[:, 0:D_WIDTH])
        i = jax.nn.sigmoid(pre[:, D_WIDTH:2 * D_WIDTH])
        log_a = (-LRU_C) * r * softplus[d:d + 1]
        a = jnp.exp(log_a)
        b = jnp.sqrt(-jnp.tanh(log_a) * (a * a + 1.0)) * (i * xc)
        return a, b

    a, b = gates(0, xf_ref, xfp_ref, xfn_ref, j)
    step = 1
    while step < tc:
        keep = rows >= step
        a_sh = jnp.where(keep, pltpu.roll(a, step, 0), 1.0)
        b_sh = jnp.where(keep, pltpu.roll(b, step, 0), 0.0)
        b = a * b_sh + b
        a = a * a_sh
        step *= 2
    h = b + a * carry_ref[0:1]
    hf_ref[0] = h
    carry_ref[0:1] = h[tc - 1:tc]

    a, b = gates(1, xb_ref, xbp_ref, xbn_ref, nc - 1 - j)
    step = 1
    while step < tc:
        keep = rows < tc - step
        a_sh = jnp.where(keep, pltpu.roll(a, tc - step, 0), 1.0)
        b_sh = jnp.where(keep, pltpu.roll(b, tc - step, 0), 0.0)
        b = a * b_sh + b
        a = a * a_sh
        step *= 2
    h = b + a * carry_ref[1:2]
    hb_ref[0] = h
    carry_ref[1:2] = h[0:1]


def _lru_call(dx, conv_w, conv_b, w_gates, b_gates, lam, tc):
    bsz, seq, w = dx.shape
    nc = seq // tc
    hb = tc // 8
    nh = seq // 8
    cur = lambda f: pl.BlockSpec((1, tc, w), f)
    halo = lambda f: pl.BlockSpec((1, 8, w), f)
    full = lambda a: pl.BlockSpec(a.shape, lambda b, j: (0,) * a.ndim)
    mirror = lambda j: nc - 1 - j
    return pl.pallas_call(
        functools.partial(_lru_kernel, tc=tc),
        grid=(bsz, nc),
        in_specs=[cur(lambda b, j: (b, j, 0)),
                  halo(lambda b, j: (b, jnp.maximum(j * hb - 1, 0), 0)),
                  halo(lambda b, j: (b, jnp.minimum((j + 1) * hb, nh - 1), 0)),
                  cur(lambda b, j: (b, mirror(j), 0)),
                  halo(lambda b, j: (b, jnp.maximum(mirror(j) * hb - 1, 0), 0)),
                  halo(lambda b, j: (b, jnp.minimum((mirror(j) + 1) * hb, nh - 1), 0)),
                  full(conv_w), full(conv_b), full(w_gates), full(b_gates), full(lam)],
        out_specs=[cur(lambda b, j: (b, j, 0)), cur(lambda b, j: (b, mirror(j), 0))],
        out_shape=[jax.ShapeDtypeStruct((bsz, seq, w), F32)] * 2,
        scratch_shapes=[pltpu.VMEM((tc + 16, w), F32), pltpu.VMEM((2, w), F32)],
        compiler_params=_params(("parallel", "arbitrary")),
        name="rg_lru",
    )(dx, dx, dx, dx, dx, dx, conv_w, conv_b, w_gates, b_gates, lam)


def _merge_kernel(x_ref, sh1_ref, sc1_ref, g1_ref, sh2_ref, sc2_ref,
                  oa_ref, ob_ref, oc_ref, hf_ref, hb_ref, dg_ref,
                  wg_ref, bg_ref, wbr_ref, wo_ref, lng_ref, lnb_ref,
                  x1_ref, h2_ref):
    x = x_ref[0]
    h = (x * (1.0 + sc1_ref[0]) + sh1_ref[0]).astype(BF16)
    od = ((hf_ref[0] + hb_ref[0]) * _gelu(dg_ref[0])).astype(BF16)
    branches = (oa_ref[0], ob_ref[0], oc_ref[0], od)
    mixed = None
    for n in range(N_BRANCH):
        cols = slice(n * D_MODEL, (n + 1) * D_MODEL)
        gate = jax.nn.sigmoid(jnp.dot(h, wg_ref[:, cols], preferred_element_type=F32) + bg_ref[:, cols])
        term = gate * jnp.dot(branches[n], wbr_ref[n], preferred_element_type=F32)
        mixed = term if mixed is None else mixed + term
    mix = jnp.dot(mixed.astype(BF16), wo_ref[...], preferred_element_type=F32)
    x1 = _layer_norm(ALPHA * x + g1_ref[0] * mix, lng_ref[...], lnb_ref[...])
    x1_ref[0] = x1
    h2_ref[0] = (x1 * (1.0 + sc2_ref[0]) + sh2_ref[0]).astype(BF16)


def _merge_call(x, mods, oa, ob, oc, hf, hb, dg, wg, bg, wbr, wo, lng, lnb, tm):
    bsz, seq, d = x.shape
    tok = lambda w: pl.BlockSpec((1, tm, w), lambda b, i: (b, i, 0))
    vec = pl.BlockSpec((1, 1, d), lambda b, i: (b, 0, 0))
    full = lambda a: pl.BlockSpec(a.shape, lambda b, i: (0,) * a.ndim)
    return pl.pallas_call(
        _merge_kernel,
        grid=(bsz, seq // tm),
        in_specs=[tok(d)] + [vec] * 5 + [tok(BRANCH_W)] * 6
                 + [full(wg), full(bg), full(wbr), full(wo), full(lng), full(lnb)],
        out_specs=[tok(d), tok(d)],
        out_shape=[jax.ShapeDtypeStruct((bsz, seq, d), F32),
                   jax.ShapeDtypeStruct((bsz, seq, d), BF16)],
        compiler_params=_params(("parallel", "arbitrary")),
        name="merge_out_ln",
    )(x, *mods, oa, ob, oc, hf, hb, dg, wg, bg, wbr, wo, lng, lnb)


def _top_values(work, count):
    vals = []
    for _ in range(count):
        mx = jnp.max(work, axis=0, keepdims=True)
        vals.append(mx)
        work = jnp.where(work == mx, -jnp.inf, work)
    return vals


def _peer_route_kernel(h2_ref, wq_ref, sk_ref, s1_ref, s2_ref, tau_ref, cc_ref):
    q = jnp.dot(h2_ref[...], wq_ref[...], preferred_element_type=F32).astype(BF16)
    k = PEER_TOPK
    for head in range(PEER_HEADS):
        top = []
        for part, s_ref in enumerate((s1_ref, s2_ref)):
            hp = 2 * head + part
            st = _dot_nt(sk_ref[hp], q[:, hp * LANES:(hp + 1) * LANES]) * LOG2_E
            s_ref[head] = st
            top.append(jnp.concatenate(_top_values(st, k), axis=0))
        sv1, sv2 = top
        cand = [sv1[0:1] + sv2]
        cand += [sv1[a:a + 1] + sv2[0:8] for a in range(1, 8)]
        cand += [sv1[8:16] + sv2[0:1]]
        best = _top_values(jnp.concatenate(cand, axis=0), k)
        m = best[0]
        zsum = jnp.zeros_like(m)
        for t in best:
            zsum = zsum + jnp.exp2(t - m)
        tau_ref[head:head + 1] = best[k - 1]
        cc_ref[head:head + 1] = m + jnp.log2(zsum)


def _peer_route_call(h2, wq, subkeys, tm):
    tokens, d = h2.shape
    score_spec = pl.BlockSpec((PEER_HEADS, PEER_KEYS, tm), lambda i: (0, 0, i))
    stat_spec = pl.BlockSpec((PEER_HEADS, tm), lambda i: (0, i))
    score_shape = jax.ShapeDtypeStruct((PEER_HEADS, PEER_KEYS, tokens), F32)
    stat_shape = jax.ShapeDtypeStruct((PEER_HEADS, tokens), F32)
    return pl.pallas_call(
        _peer_route_kernel,
        grid=(tokens // tm,),
        in_specs=[pl.BlockSpec((tm, d), lambda i: (i, 0)),
                  pl.BlockSpec(wq.shape, lambda i: (0, 0)),
                  pl.BlockSpec(subkeys.shape, lambda i: (0, 0, 0))],
        out_specs=[score_spec, score_spec, stat_spec, stat_spec],
        out_shape=[score_shape, score_shape, stat_shape, stat_shape],
        compiler_params=_params(("parallel",)),
        name="peer_route",
    )(h2, wq, subkeys)


def _peer_mix_kernel(h2_ref, ux_ref, uy_ref, vtx_ref, vty_ref, s1x_ref, s1y_ref, s2_ref, tau_ref, cc_ref,
                     x1_ref, g2_ref, lng_ref, lnb_ref, o_ref,
                     acc_ref, stx_ref, sty_ref, wx_ref, wy_ref, *, tt):
    j = pl.program_id(1)

    @pl.when(j == 0)
    def _():
        acc_ref[...] = jnp.zeros(acc_ref.shape, F32)
        sty_ref[...] = jnp.zeros(sty_ref.shape, F32)
        wx_ref[...] = jnp.zeros(wx_ref.shape, BF16)

    per_tile = SUBLANES // 2
    width = 2 * LANES

    def half_step(u_ref, st_out, st_in, s1_ref, row0, w_out, vt_ref, w_in):
        for c in range(tt // width):
            cols = slice(c * width, (c + 1) * width)
            st_out[:, cols] = _dot_nt(u_ref[...], h2_ref[cols, :])
            for half in range(width // LANES):
                lanes = slice(c * width + half * LANES, c * width + (half + 1) * LANES)
                for a in range(per_tile):
                    g = jnp.zeros((PEER_KEYS, LANES), F32)
                    for head in range(PEER_HEADS):
                        zsum = s1_ref[head, row0 + a:row0 + a + 1, lanes] + s2_ref[head, :, lanes]
                        g = g + jnp.where(zsum >= tau_ref[head:head + 1, lanes],
                                          jnp.exp2(zsum - cc_ref[head:head + 1, lanes]), 0.0)
                    rows = slice(a * PEER_KEYS, (a + 1) * PEER_KEYS)
                    w_out[rows, lanes] = (_gelu(st_in[rows, lanes]) * g).astype(BF16)
            acc_ref[:, cols] += jnp.dot(vt_ref[...], w_in[:, cols], preferred_element_type=F32)

    half_step(ux_ref, stx_ref, sty_ref, s1x_ref, per_tile, wy_ref, vtx_ref, wx_ref)
    half_step(uy_ref, sty_ref, stx_ref, s1y_ref, 0, wx_ref, vty_ref, wy_ref)

    @pl.when(j == pl.num_programs(1) - 1)
    def _():
        y = acc_ref[...].T
        o_ref[...] = _layer_norm(ALPHA * x1_ref[...] + g2_ref[0] * y, lng_ref[...], lnb_ref[...])


def _peer_mix_call(h2, u, vt, s1, s2, tau, cc, x1, gate2, lng, lnb, seq, tt):
    tokens, d = h2.shape
    tn = (SUBLANES // 2) * PEER_KEYS
    n_tiles = u.shape[0] // tn
    pairs = n_tiles // 2
    per_batch = seq // tt
    assert u.shape[0] % (2 * tn) == 0 and tt % (2 * LANES) == 0
    clamp = lambda t: jnp.clip(t, 0, n_tiles - 1)
    u_spec = lambda f: pl.BlockSpec((tn, d), lambda i, j: (clamp(f(j)), 0))
    vt_spec = lambda f: pl.BlockSpec((d, tn), lambda i, j: (0, clamp(f(j))))
    s1_spec = lambda f: pl.BlockSpec((PEER_HEADS, SUBLANES, tt), lambda i, j: (0, jnp.clip(f(j), 0, pairs - 1), i))
    return pl.pallas_call(
        functools.partial(_peer_mix_kernel, tt=tt),
        grid=(tokens // tt, pairs + 1),
        in_specs=[pl.BlockSpec((tt, d), lambda i, j: (i, 0)),
                  u_spec(lambda j: 2 * j), u_spec(lambda j: 2 * j + 1),
                  vt_spec(lambda j: 2 * j - 2), vt_spec(lambda j: 2 * j - 1),
                  s1_spec(lambda j: j - 1), s1_spec(lambda j: j),
                  pl.BlockSpec((PEER_HEADS, PEER_KEYS, tt), lambda i, j: (0, 0, i)),
                  pl.BlockSpec((PEER_HEADS, tt), lambda i, j: (0, i)),
                  pl.BlockSpec((PEER_HEADS, tt), lambda i, j: (0, i)),
                  pl.BlockSpec((tt, d), lambda i, j: (i, 0)),
                  pl.BlockSpec((1, 1, d), lambda i, j: (i // per_batch, 0, 0)),
                  pl.BlockSpec((1, d), lambda i, j: (0, 0)),
                  pl.BlockSpec((1, d), lambda i, j: (0, 0))],
        out_specs=pl.BlockSpec((tt, d), lambda i, j: (i, 0)),
        out_shape=jax.ShapeDtypeStruct((tokens, d), F32),
        scratch_shapes=[pltpu.VMEM((d, tt), F32),
                        pltpu.VMEM((tn, tt), F32), pltpu.VMEM((tn, tt), F32),
                        pltpu.VMEM((tn, tt), BF16), pltpu.VMEM((tn, tt), BF16)],
        compiler_params=_params(("parallel", "arbitrary")),
        name="peer_mix",
    )(h2, u, u, vt, vt, s1, s1, s2, tau, cc, x1, gate2, lng, lnb)


def _rope_tables(seq):
    inv = 1.0 / (ROPE_THETA ** (jnp.arange(0, HEAD_DIM, 2, dtype=F32) / HEAD_DIM))
    ang = jnp.arange(seq, dtype=F32)[:, None] * inv[None, :]
    cos, sin = jnp.cos(ang), jnp.sin(ang)
    zero = jnp.zeros_like(sin)
    reps = LANES // HEAD_DIM
    cos_t = jnp.tile(jnp.concatenate([cos, cos], axis=1), (1, reps))
    sin_lo = jnp.tile(jnp.concatenate([-sin, zero], axis=1), (1, reps))
    sin_hi = jnp.tile(jnp.concatenate([zero, sin], axis=1), (1, reps))
    return cos_t, sin_lo, sin_hi


def _block_diag(w):
    nb, bs, _ = w.shape
    eye = jnp.eye(nb, dtype=w.dtype)
    return (eye[:, None, :, None] * w[:, :, None, :]).reshape(nb * bs, nb * bs)


def _tile(n, want):
    t = min(n, want)
    assert n % t == 0
    return t


def kernel(x, c, w_ada, b_ada, w_in, b_gate, a_sink, b_rpb, c_lambda, c_norm_g, d_conv_w, d_conv_b, d_wa, d_ba, d_wx, d_bx, d_lam, w_branch, w_out, ln_g, ln_b, p_wq, p_subkeys, p_u, p_v):
    bsz, seq, d = x.shape
    tokens = bsz * seq
    rows = seq // GRID_W
    assert d == D_MODEL and seq % (B_QROWS * GRID_W) == 0 and rows // B_QROWS >= 3

    c_pad = jnp.pad(c, ((0, 8 - bsz), (0, 0)))
    mod = _ada_call(c_pad, w_ada, b_ada)[:, :bsz].reshape(DEPTH, bsz, 6, 1, d)
    cos_t, sin_lo, sin_hi = _rope_tables(seq)
    bias_tiles = _nbr_bias_tables(b_rpb, rows)

    head_cols = np.concatenate([np.arange(h * HEAD_DIM, (h + 1) * HEAD_DIM) for h in A_HEAD_ORDER])
    w_in16 = w_in.astype(BF16)
    wg16 = w_in16[:, :, OFF_G:]
    wbr16 = w_branch.astype(BF16)
    wo16 = w_out.astype(BF16)
    wq16 = p_wq.astype(BF16)
    sk16 = p_subkeys.astype(BF16).reshape(DEPTH, 2 * PEER_HEADS, PEER_KEYS, PEER_QDIM // 2)
    u16 = p_u.astype(BF16)
    vt16 = jnp.swapaxes(p_v, 1, 2).astype(BF16)

    for l in range(DEPTH):
        mods = [mod[l, :, i] for i in range(6)]
        shift1, scale1, gate1, shift2, scale2, gate2 = mods
        wl = w_in16[l]
        wa = jnp.concatenate([wl[:, head_cols], wl[:, A_Q:OFF_B]], axis=1)
        aq, akv, bq, bk, bv, cq, ck, cv, dx, dg = _inproj_call(
            x, shift1, scale1, wa, wl[:, OFF_B:OFF_C], wl[:, OFF_C:OFF_D], wl[:, OFF_D:OFF_G],
            cos_t, sin_lo, sin_hi, _tile(seq, 512))

        oa = _window_call(a_sink[l], aq, akv)
        ob = _nbr_call(bq, bk, bv, bias_tiles, l)
        lam_init = 0.8 - 0.6 * math.exp(-0.3 * l)
        oc = _diff_call(c_lambda[l], c_norm_g[l], cq, ck, cv, lam_init,
                        _tile(seq, DIFF_TQ), _tile(seq, DIFF_TK))
        w_gates = jnp.stack([
            jnp.concatenate([_block_diag(d_wa[l, dr]), _block_diag(d_wx[l, dr])], axis=1)
            for dr in range(2)]).astype(BF16)
        b_gates = jnp.concatenate([d_ba[l], d_bx[l]], axis=1)
        hf, hb = _lru_call(dx, d_conv_w[l], d_conv_b[l].reshape(1, D_WIDTH), w_gates, b_gates,
                           d_lam[l], _tile(seq, 256))

        wbr = wbr16[l]
        wbr = wbr.at[0].set(wbr[0].reshape(A_HEADS, HEAD_DIM, d)[np.array(A_HEAD_ORDER)].reshape(BRANCH_W, d))
        x1, h2 = _merge_call(x, [shift1, scale1, gate1, shift2, scale2], oa, ob, oc, hf, hb, dg,
                             wg16[l], b_gate[l].reshape(1, -1), wbr, wo16[l],
                             ln_g[l, 0].reshape(1, d), ln_b[l, 0].reshape(1, d), _tile(seq, 256))

        h2f = h2.reshape(tokens, d)
        s1, s2, tau, cc = _peer_route_call(h2f, wq16[l], sk16[l], _tile(tokens, 512))
        xn = _peer_mix_call(h2f, u16[l], vt16[l], s1, s2, tau, cc, x1.reshape(tokens, d), gate2,
                            ln_g[l, 1].reshape(1, d), ln_b[l, 1].reshape(1, d),
                            seq, _tile(seq, 1024))
        x = xn.reshape(bsz, seq, d)
    return x
```

```python
import functools
import math

import numpy as np
import jax
import jax.numpy as jnp
from jax import lax
from jax.experimental import pallas as pl
from jax.experimental.pallas import tpu as pltpu

F32 = jnp.float32
BF16 = jnp.bfloat16

LANES = 128
SUBLANES = 8
VMEM_LIMIT = 56 * 1024 * 1024

D_MODEL = 1024
DEPTH = 4
GRID_W = 64
HEAD_DIM = 64
ROPE_THETA = 10000.0
NEG_INF = -1e30
LN_EPS = 1e-5
A_HEADS = 8
A_KV_HEADS = 2
A_WINDOW = 128
A_BLOCK = 128
B_HEADS = 8
B_KH = 8
B_KW = 16
C_HEADS = 4
C_VDIM = 2 * HEAD_DIM
D_WIDTH = 512
D_BLOCKS = 8
D_CONV = 4
LRU_C = 8.0
N_BRANCH = 4
BRANCH_W = 512
PEER_HEADS = 8
PEER_KEYS = 128
PEER_N = PEER_KEYS * PEER_KEYS
PEER_QDIM = 256
PEER_TOPK = 16
ALPHA = (2.0 * DEPTH) ** 0.25
LOG2_E = math.log2(math.e)

A_Q = A_HEADS * HEAD_DIM
A_KV = A_KV_HEADS * HEAD_DIM
B_QKV = B_HEADS * HEAD_DIM
C_QK = C_HEADS * 2 * HEAD_DIM
C_V = C_HEADS * C_VDIM
OFF_A = 0
OFF_B = A_Q + 2 * A_KV
OFF_C = OFF_B + 3 * B_QKV
OFF_D = OFF_C + 2 * C_QK + C_V
OFF_G = OFF_D + 2 * D_WIDTH

DIFF_TQ, DIFF_TK = 512, 2048

B_QROWS = 4
B_KROWS = B_QROWS + B_KH


def _params(sem):
    return pltpu.CompilerParams(dimension_semantics=sem, vmem_limit_bytes=VMEM_LIMIT)


def _gelu(x):
    return x * (0.5 * (1.0 + jnp.tanh(math.sqrt(2.0 / math.pi) * (x + 0.044715 * (x * x * x)))))


def _layer_norm(y, g, b):
    mu = jnp.mean(y, axis=-1, keepdims=True)
    yc = y - mu
    var = jnp.mean(yc * yc, axis=-1, keepdims=True)
    return yc * lax.rsqrt(var + LN_EPS) * g + b


def _dot_nt(a, b):
    return lax.dot_general(a, b, (((1,), (1,)), ((), ())), preferred_element_type=F32)


def _low_half_mask():
    return lax.broadcasted_iota(jnp.int32, (1, LANES), 1) < HEAD_DIM


def _ada_kernel(c_ref, w_ref, b_ref, o_ref):
    c = c_ref[...]
    ca = c * jax.nn.sigmoid(c)
    o_ref[0] = jnp.dot(ca, w_ref[0], precision=lax.Precision.HIGHEST,
                       preferred_element_type=F32) + b_ref[0]


def _ada_call(c_pad, w_ada, b_ada):
    depth, d, cols = w_ada.shape
    rows = c_pad.shape[0]
    tn = 1024
    return pl.pallas_call(
        _ada_kernel,
        grid=(depth, cols // tn),
        in_specs=[pl.BlockSpec((rows, d), lambda l, j: (0, 0)),
                  pl.BlockSpec((1, d, tn), lambda l, j: (l, 0, j)),
                  pl.BlockSpec((1, 1, tn), lambda l, j: (l, 0, j))],
        out_specs=pl.BlockSpec((1, rows, tn), lambda l, j: (l, 0, j)),
        out_shape=jax.ShapeDtypeStruct((depth, rows, cols), F32),
        compiler_params=_params(("arbitrary", "arbitrary")),
        name="adaln_mod",
    )(c_pad, w_ada, b_ada.reshape(depth, 1, cols))


def _inproj_kernel(x_ref, sh_ref, sc_ref, w_ref, cos_ref, slo_ref, shi_ref,
                   aq_ref, akv_ref, bq_ref, bk_ref, bv_ref, cq_ref, ck_ref, cv_ref,
                   dx_ref, dg_ref):
    h = (x_ref[0] * (1.0 + sc_ref[...]) + sh_ref[...]).astype(BF16)
    cos = cos_ref[...]
    slo = slo_ref[...]
    shi = shi_ref[...]
    qscale = HEAD_DIM ** -0.5

    def rope(v):
        return (v * cos + pltpu.roll(v, LANES - HEAD_DIM // 2, 1) * slo
                + pltpu.roll(v, HEAD_DIM // 2, 1) * shi)

    def tiles(y, start, n):
        return [y[:, (start + i) * LANES:(start + i + 1) * LANES] for i in range(n)]

    ya = jnp.dot(h, w_ref[:, OFF_A:OFF_B], preferred_element_type=F32)
    for i, t in enumerate(tiles(ya, 0, A_Q // LANES)):
        aq_ref[0, :, i * LANES:(i + 1) * LANES] = (rope(t) * qscale).astype(BF16)
    ak = rope(ya[:, A_Q:A_Q + LANES])
    av = ya[:, A_Q + LANES:A_Q + 2 * LANES]
    for i, t in enumerate((ak, av, pltpu.roll(ak, HEAD_DIM, 1), pltpu.roll(av, HEAD_DIM, 1))):
        akv_ref[0, :, i * LANES:(i + 1) * LANES] = t.astype(BF16)

    yb = jnp.dot(h, w_ref[:, OFF_B:OFF_C], preferred_element_type=F32)
    bq_ref[0] = (yb[:, 0:B_QKV] * qscale).astype(BF16)
    bk_ref[0] = yb[:, B_QKV:2 * B_QKV].astype(BF16)
    bv_ref[0] = yb[:, 2 * B_QKV:3 * B_QKV].astype(BF16)

    yc = jnp.dot(h, w_ref[:, OFF_C:OFF_D], preferred_element_type=F32)
    nt = C_QK // LANES
    for i, t in enumerate(tiles(yc, 0, nt)):
        cq_ref[0, :, i * LANES:(i + 1) * LANES] = (rope(t) * qscale).astype(BF16)
    for i, t in enumerate(tiles(yc, nt, nt)):
        ck_ref[0, :, i * LANES:(i + 1) * LANES] = rope(t).astype(BF16)
    cv_ref[0] = yc[:, 2 * C_QK:2 * C_QK + C_V].astype(BF16)

    yd = jnp.dot(h, w_ref[:, OFF_D:OFF_G], preferred_element_type=F32)
    dx_ref[0] = yd[:, 0:D_WIDTH]
    dg_ref[0] = yd[:, D_WIDTH:2 * D_WIDTH]


def _mod_spec(layer, which, bsz, d):
    return pl.BlockSpec((None, 1, d), lambda b, *_: ((layer * bsz + b) * 6 + which, 0, 0))


def _inproj_call(x, mod, w_front, layer, cos_t, sin_lo, sin_hi, tm):
    bsz, seq, d = x.shape
    tok = lambda w: pl.BlockSpec((1, tm, w), lambda b, i: (b, i, 0))
    tab = pl.BlockSpec((tm, LANES), lambda b, i: (i, 0))
    outs = [(A_Q, BF16), (4 * LANES, BF16), (B_QKV, BF16), (B_QKV, BF16), (B_QKV, BF16),
            (C_QK, BF16), (C_QK, BF16), (C_V, BF16), (D_WIDTH, F32), (D_WIDTH, F32)]
    return pl.pallas_call(
        _inproj_kernel,
        grid=(bsz, seq // tm),
        in_specs=[tok(d), _mod_spec(layer, 0, bsz, d), _mod_spec(layer, 1, bsz, d),
                  pl.BlockSpec((None,) + w_front.shape[1:], lambda b, i: (layer, 0, 0)), tab, tab, tab],
        out_specs=[tok(w) for w, _ in outs],
        out_shape=[jax.ShapeDtypeStruct((bsz, seq, w), dt) for w, dt in outs],
        compiler_params=_params(("parallel", "arbitrary")),
        name="in_proj",
    )(x, mod, mod, w_front, cos_t, sin_lo, sin_hi)


def _window_kernel(sink_ref, q_ref, kvp_ref, kvc_ref, kvn_ref, o_ref, *, seq, layer):
    n = pl.program_id(1)
    blk = A_BLOCK
    kv = jnp.concatenate([kvp_ref[0], kvc_ref[0], kvn_ref[0]], axis=0)
    k_tiles = (kv[:, 0:LANES], kv[:, 2 * LANES:3 * LANES])
    v_tiles = (kv[:, LANES:2 * LANES], kv[:, 3 * LANES:4 * LANES])
    qi = lax.broadcasted_iota(jnp.int32, (2 * blk, 3 * blk), 0) % blk
    kj = lax.broadcasted_iota(jnp.int32, (2 * blk, 3 * blk), 1)
    rel = kj - blk - qi
    kpos = n * blk + kj - blk
    valid = (jnp.abs(rel) <= A_WINDOW) & (kpos >= 0) & (kpos < seq)
    low_rows = lax.broadcasted_iota(jnp.int32, (2 * blk, 1), 0) < blk
    low = _low_half_mask()
    heads_per_kv = A_HEADS // A_KV_HEADS
    for pair in range(A_HEADS // 2):
        kv_head = (2 * pair) // heads_per_kv
        lo_src, hi_src = kv_head, 1 - kv_head
        qp = q_ref[0, :, pair * LANES:(pair + 1) * LANES]
        zero = jnp.zeros_like(qp)
        s = jnp.concatenate([_dot_nt(jnp.where(low, qp, zero), k_tiles[lo_src]),
                             _dot_nt(jnp.where(low, zero, qp), k_tiles[hi_src])], axis=0)
        s = jnp.where(valid, s, NEG_INF)
        sk = jnp.where(low_rows, sink_ref[layer, 2 * pair], sink_ref[layer, 2 * pair + 1])
        m = jnp.maximum(jnp.max(s, axis=-1, keepdims=True), sk)
        p = jnp.exp(s - m)
        p = (p / (jnp.sum(p, axis=-1, keepdims=True) + jnp.exp(sk - m))).astype(BF16)
        o_lo = jnp.dot(p[0:blk], v_tiles[lo_src], preferred_element_type=F32)
        o_hi = jnp.dot(p[blk:], v_tiles[hi_src], preferred_element_type=F32)
        o_ref[0, :, pair * LANES:(pair + 1) * LANES] = jnp.where(low, o_lo, o_hi).astype(BF16)


def _window_call(sink, aq, akv, layer):
    bsz, seq, _ = aq.shape
    nb = seq // A_BLOCK
    kvspec = lambda f: pl.BlockSpec((1, A_BLOCK, 4 * LANES), f)
    return pl.pallas_call(
        functools.partial(_window_kernel, seq=seq, layer=layer),
        grid=(bsz, nb),
        in_specs=[pl.BlockSpec(memory_space=pltpu.SMEM),
                  pl.BlockSpec((1, A_BLOCK, A_Q), lambda b, n: (b, n, 0)),
                  kvspec(lambda b, n: (b, jnp.maximum(n - 1, 0), 0)),
                  kvspec(lambda b, n: (b, n, 0)),
                  kvspec(lambda b, n: (b, jnp.minimum(n + 1, nb - 1), 0))],
        out_specs=pl.BlockSpec((1, A_BLOCK, A_Q), lambda b, n: (b, n, 0)),
        out_shape=jax.ShapeDtypeStruct((bsz, seq, A_Q), BF16),
        compiler_params=_params(("parallel", "arbitrary")),
        name="window_gqa",
    )(sink, aq, akv, akv, akv)


def _nbr_bias_tables(rpb, rows):
    groups = rows // B_QROWS
    qr = np.arange(B_QROWS)
    kr = np.arange(B_KROWS)
    row_onehot = np.zeros((3, B_QROWS, B_KROWS, 2 * B_KH - 1), np.float32)
    row_valid = np.zeros((3, B_QROWS, B_KROWS), bool)
    for var, g in enumerate((0, 1, groups - 1)):
        base = int(np.clip(g - 1, 0, groups - 3)) * B_QROWS
        r = g * B_QROWS + qr
        rstart = np.clip(r - B_KH // 2, 0, rows - B_KH)
        krow = base + kr
        ok = (krow[None, :] >= rstart[:, None]) & (krow[None, :] < rstart[:, None] + B_KH)
        drow = np.clip(krow[None, :] - r[:, None] + (B_KH - 1), 0, 2 * B_KH - 2)
        row_valid[var] = ok
        row_onehot[var, qr[:, None], kr[None, :], drow] = 1.0
    qc = np.arange(GRID_W)
    kc = np.arange(GRID_W)
    cstart = np.clip(qc - B_KW // 2, 0, GRID_W - B_KW)
    col_valid = (kc[None, :] >= cstart[:, None]) & (kc[None, :] < cstart[:, None] + B_KW)
    dcol = np.clip(kc[None, :] - qc[:, None], 1 - B_KW, B_KW - 1) + (B_KW - 1)
    col_onehot = np.zeros((GRID_W, GRID_W, 2 * B_KW - 1), np.float32)
    col_onehot[qc[:, None], kc[None, :], dcol] = 1.0
    valid = row_valid[:, :, None, :, None] & col_valid[None, None, :, None, :]
    hi = lax.Precision.HIGHEST
    by_col = jnp.einsum('lhab,qkb->lhaqk', rpb, jnp.asarray(col_onehot), precision=hi)
    tiles = jnp.einsum('lhaqk,vrsa->lvhrqsk', by_col, jnp.asarray(row_onehot), precision=hi)
    tiles = jnp.where(jnp.asarray(valid)[None, :, None], tiles, NEG_INF)
    depth = rpb.shape[0]
    return tiles.reshape(depth, 3, B_HEADS, B_QROWS * GRID_W, B_KROWS * GRID_W)


def _nbr_kernel(q_ref, k0_ref, k1_ref, k2_ref, v0_ref, v1_ref, v2_ref, bias_ref, o_ref):
    nq = B_QROWS * GRID_W
    k = jnp.concatenate([k0_ref[0], k1_ref[0], k2_ref[0]], axis=0)
    v = jnp.concatenate([v0_ref[0], v1_ref[0], v2_ref[0]], axis=0)
    low = _low_half_mask()
    for pair in range(B_HEADS // 2):
        cols = slice(pair * LANES, (pair + 1) * LANES)
        qp = q_ref[0, :, cols]
        zero = jnp.zeros_like(qp)
        q2 = jnp.concatenate([jnp.where(low, qp, zero), jnp.where(low, zero, qp)], axis=0)
        bias = jnp.concatenate([bias_ref[0, 0, 2 * pair], bias_ref[0, 0, 2 * pair + 1]], axis=0)
        s = _dot_nt(q2, k[:, cols]) + bias
        m = jnp.max(s, axis=-1, keepdims=True)
        p = jnp.exp(s - m)
        p = p / jnp.sum(p, axis=-1, keepdims=True)
        o2 = jnp.dot(p.astype(BF16), v[:, cols], preferred_element_type=F32)
        o_ref[0, :, cols] = jnp.where(low, o2[0:nq], o2[nq:]).astype(BF16)


def _nbr_call(bq, bk, bv, bias_tiles, layer):
    bsz, seq, w = bq.shape
    nq = B_QROWS * GRID_W
    groups = seq // nq
    start = lambda g: jnp.clip(g - 1, 0, groups - 3)
    kvspec = lambda off: pl.BlockSpec((1, nq, w), lambda b, g: (b, start(g) + off, 0))
    variant = lambda g: jnp.where(g == 0, 0, jnp.where(g == groups - 1, 2, 1))
    return pl.pallas_call(
        _nbr_kernel,
        grid=(bsz, groups),
        in_specs=[pl.BlockSpec((1, nq, w), lambda b, g: (b, g, 0)),
                  kvspec(0), kvspec(1), kvspec(2), kvspec(0), kvspec(1), kvspec(2),
                  pl.BlockSpec((1, 1, B_HEADS, nq, B_KROWS * GRID_W),
                               lambda b, g: (layer, variant(g), 0, 0, 0))],
        out_specs=pl.BlockSpec((1, nq, w), lambda b, g: (b, g, 0)),
        out_shape=jax.ShapeDtypeStruct((bsz, seq, w), BF16),
        compiler_params=_params(("parallel", "arbitrary")),
        name="neighbourhood_attn",
    )(bq, bk, bk, bk, bv, bv, bv, bias_tiles)


def _diff_kernel(lam_ref, ng_ref, q_ref, k_ref, v_ref, o_ref, m_ref, acc_ref, *, lam_init):
    j = pl.program_id(3)
    tq = q_ref.shape[1]
    tk = k_ref.shape[1]

    @pl.when(j == 0)
    def _():
        m_ref[...] = jnp.full(m_ref.shape, -jnp.inf, F32)
        acc_ref[...] = jnp.zeros(acc_ref.shape, F32)

    low = _low_half_mask()
    q = q_ref[0]
    k = k_ref[0]
    zero = jnp.zeros_like(q)
    v_ext = jnp.concatenate([v_ref[0], jnp.ones((tk, LANES), BF16)], axis=1)
    scores = [_dot_nt(qm, k) for qm in (jnp.where(low, q, zero), jnp.where(low, zero, q))]
    for mp, s in enumerate(scores):
        tiles = [s[:, c * LANES:(c + 1) * LANES] for c in range(tk // LANES)]
        part = tiles[0]
        for t in tiles[1:]:
            part = jnp.maximum(part, t)
        m_old = m_ref[mp]
        m_new = jnp.maximum(m_old, jnp.broadcast_to(jnp.max(part, axis=-1, keepdims=True), (tq, LANES)))
        alpha = jnp.exp(m_old - m_new)
        p = jnp.concatenate([jnp.exp(t - m_new) for t in tiles], axis=1).astype(BF16)
        acc_ref[mp] = (jnp.concatenate([alpha, alpha], axis=1) * acc_ref[mp]
                       + jnp.dot(p, v_ext, preferred_element_type=F32))
        m_ref[mp] = m_new

    @pl.when(j == pl.num_programs(3) - 1)
    def _():
        lf = lam_ref[...]
        lam = (jnp.exp(jnp.sum(lf[0:1] * lf[1:2], axis=-1, keepdims=True))
               - jnp.exp(jnp.sum(lf[2:3] * lf[3:4], axis=-1, keepdims=True)) + lam_init)
        a1 = acc_ref[0]
        a2 = acc_ref[1]
        o = a1[:, 0:C_VDIM] / a1[:, C_VDIM:] - lam * (a2[:, 0:C_VDIM] / a2[:, C_VDIM:])
        o = o * lax.rsqrt(jnp.mean(o * o, axis=-1, keepdims=True) + LN_EPS)
        o_ref[0] = (o * ng_ref[...] * (1.0 - lam_init)).astype(BF16)


def _diff_call(lam, norm_g, cq, ck, cv, layer, tq, tk):
    bsz, seq, _ = cq.shape
    lam_init = 0.8 - 0.6 * math.exp(-0.3 * layer)
    return pl.pallas_call(
        functools.partial(_diff_kernel, lam_init=lam_init),
        grid=(bsz, C_HEADS, seq // tq, seq // tk),
        in_specs=[pl.BlockSpec((None, 4, HEAD_DIM), lambda b, h, i, j: (layer, 0, 0)),
                  pl.BlockSpec((None, 1, C_VDIM), lambda b, h, i, j: (layer, 0, h)),
                  pl.BlockSpec((1, tq, LANES), lambda b, h, i, j: (b, i, h)),
                  pl.BlockSpec((1, tk, LANES), lambda b, h, i, j: (b, j, h)),
                  pl.BlockSpec((1, tk, C_VDIM), lambda b, h, i, j: (b, j, h))],
        out_specs=pl.BlockSpec((1, tq, C_VDIM), lambda b, h, i, j: (b, i, h)),
        out_shape=jax.ShapeDtypeStruct((bsz, seq, C_V), BF16),
        scratch_shapes=[pltpu.VMEM((2, tq, LANES), F32), pltpu.VMEM((2, tq, C_VDIM + LANES), F32)],
        compiler_params=_params(("parallel", "parallel", "parallel", "arbitrary")),
        name="diff_attn",
    )(lam, norm_g, cq, ck, cv)


def _lru_kernel(xf_ref, xfp_ref, xfn_ref, xb_ref, xbp_ref, xbn_ref, cw_ref, cb_ref,
                w_ref, bias_ref, lam_ref, hf_ref, hb_ref, ext_ref, carry_ref, *, tc):
    j = pl.program_id(1)
    nc = pl.num_programs(1)
    halo = 8
    left = D_CONV // 2

    @pl.when(j == 0)
    def _():
        carry_ref[...] = jnp.zeros(carry_ref.shape, F32)

    z = -lam_ref[...]
    softplus = jnp.maximum(z, 0.0) + jnp.log1p(jnp.exp(-jnp.abs(z)))
    rows = lax.broadcasted_iota(jnp.int32, (tc, 1), 0)

    def gates(d, cur_ref, prev_ref, next_ref, chunk):
        ext_ref[0:halo] = jnp.where(chunk > 0, prev_ref[0], 0.0)
        ext_ref[halo:halo + tc] = cur_ref[0]
        ext_ref[halo + tc:2 * halo + tc] = jnp.where(chunk < nc - 1, next_ref[0], 0.0)
        xc = cb_ref[...]
        for t in range(D_CONV):
            xc = xc + cw_ref[t:t + 1] * ext_ref[pl.ds(halo - left + t, tc), :]
        pre = jnp.dot(xc.astype(BF16), w_ref[d], preferred_element_type=F32) + bias_ref[d:d + 1]
        r = jax.nn.sigmoid(pre[:, 0:D_WIDTH])
        i = jax.nn.sigmoid(pre[:, D_WIDTH:2 * D_WIDTH])
        log_a = (-LRU_C) * r * softplus[d:d + 1]
        a = jnp.exp(log_a)
        b = jnp.sqrt(-jnp.tanh(log_a) * (a * a + 1.0)) * (i * xc)
        return a, b

    a, b = gates(0, xf_ref, xfp_ref, xfn_ref, j)
    step = 1
    while step < tc:
        keep = rows >= step
        a_sh = jnp.where(keep, pltpu.roll(a, step, 0), 1.0)
        b_sh = jnp.where(keep, pltpu.roll(b, step, 0), 0.0)
        b = a * b_sh + b
        a = a * a_sh
        step *= 2
    h = b + a * carry_ref[0:1]
    hf_ref[0] = h
    carry_ref[0:1] = h[tc - 1:tc]

    a, b = gates(1, xb_ref, xbp_ref, xbn_ref, nc - 1 - j)
    step = 1
    while step < tc:
        keep = rows < tc - step
        a_sh = jnp.where(keep, pltpu.roll(a, tc - step, 0), 1.0)
        b_sh = jnp.where(keep, pltpu.roll(b, tc - step, 0), 0.0)
        b = a * b_sh + b
        a = a * a_sh
        step *= 2
    h = b + a * carry_ref[1:2]
    hb_ref[0] = h
    carry_ref[1:2] = h[0:1]


def _lru_call(dx, conv_w, conv_b, w_gates, b_gates, lam, layer, tc):
    bsz, seq, w = dx.shape
    nc = seq // tc
    hb = tc // 8
    nh = seq // 8
    cur = lambda f: pl.BlockSpec((1, tc, w), f)
    halo = lambda f: pl.BlockSpec((1, 8, w), f)
    full = lambda a: pl.BlockSpec((None,) + a.shape[1:], lambda b, j: (layer,) + (0,) * (a.ndim - 1))
    mirror = lambda j: nc - 1 - j
    return pl.pallas_call(
        functools.partial(_lru_kernel, tc=tc),
        grid=(bsz, nc),
        in_specs=[cur(lambda b, j: (b, j, 0)),
                  halo(lambda b, j: (b, jnp.maximum(j * hb - 1, 0), 0)),
                  halo(lambda b, j: (b, jnp.minimum((j + 1) * hb, nh - 1), 0)),
                  cur(lambda b, j: (b, mirror(j), 0)),
                  halo(lambda b, j: (b, jnp.maximum(mirror(j) * hb - 1, 0), 0)),
                  halo(lambda b, j: (b, jnp.minimum((mirror(j) + 1) * hb, nh - 1), 0)),
                  full(conv_w), full(conv_b), full(w_gates), full(b_gates), full(lam)],
        out_specs=[cur(lambda b, j: (b, j, 0)), cur(lambda b, j: (b, mirror(j), 0))],
        out_shape=[jax.ShapeDtypeStruct((bsz, seq, w), F32)] * 2,
        scratch_shapes=[pltpu.VMEM((tc + 16, w), F32), pltpu.VMEM((2, w), F32)],
        compiler_params=_params(("parallel", "arbitrary")),
        name="rg_lru",
    )(dx, dx, dx, dx, dx, dx, conv_w, conv_b, w_gates, b_gates, lam)


def _merge_kernel(x_ref, sh1_ref, sc1_ref, g1_ref, sh2_ref, sc2_ref,
                  oa_ref, ob_ref, oc_ref, hf_ref, hb_ref, dg_ref,
                  wg_ref, bg_ref, wbr_ref, wo_ref, lng_ref, lnb_ref,
                  x1_ref, h2_ref):
    x = x_ref[0]
    h = (x * (1.0 + sc1_ref[...]) + sh1_ref[...]).astype(BF16)
    od = ((hf_ref[0] + hb_ref[0]) * _gelu(dg_ref[0])).astype(BF16)
    branches = (oa_ref[0], ob_ref[0], oc_ref[0], od)
    mixed = None
    for n in range(N_BRANCH):
        cols = slice(n * D_MODEL, (n + 1) * D_MODEL)
        gate = jax.nn.sigmoid(jnp.dot(h, wg_ref[:, cols], preferred_element_type=F32) + bg_ref[:, cols])
        term = gate * jnp.dot(branches[n], wbr_ref[n], preferred_element_type=F32)
        mixed = term if mixed is None else mixed + term
    mix = jnp.dot(mixed.astype(BF16), wo_ref[...], preferred_element_type=F32)
    x1 = _layer_norm(ALPHA * x + g1_ref[...] * mix, lng_ref[...], lnb_ref[...])
    x1_ref[0] = x1
    h2_ref[0] = (x1 * (1.0 + sc2_ref[...]) + sh2_ref[...]).astype(BF16)


def _merge_call(x, mod, oa, ob, oc, hf, hb, dg, wg, bg, wbr, wo, lng, lnb, layer, tm):
    bsz, seq, d = x.shape
    tok = lambda w: pl.BlockSpec((1, tm, w), lambda b, i: (b, i, 0))
    full = lambda a: pl.BlockSpec((None,) + a.shape[1:], lambda b, i: (layer,) + (0,) * (a.ndim - 1))
    ln = pl.BlockSpec((None, 1, d), lambda b, i: (2 * layer, 0, 0))
    return pl.pallas_call(
        _merge_kernel,
        grid=(bsz, seq // tm),
        in_specs=[tok(d)] + [_mod_spec(layer, which, bsz, d) for which in range(5)] + [tok(BRANCH_W)] * 6
                 + [full(wg), full(bg), full(wbr), full(wo), ln, ln],
        out_specs=[tok(d), tok(d)],
        out_shape=[jax.ShapeDtypeStruct((bsz, seq, d), F32),
                   jax.ShapeDtypeStruct((bsz, seq, d), BF16)],
        compiler_params=_params(("parallel", "arbitrary")),
        name="merge_out_ln",
    )(x, mod, mod, mod, mod, mod, oa, ob, oc, hf, hb, dg, wg, bg, wbr, wo, lng, lnb)


def _top_values(work, count):
    vals = []
    for _ in range(count):
        mx = jnp.max(work, axis=0, keepdims=True)
        vals.append(mx)
        work = jnp.where(work == mx, -jnp.inf, work)
    return vals


def _peer_route_kernel(h2_ref, wq_ref, sk_ref, s1_ref, s2_ref, tau_ref, cc_ref):
    q = jnp.dot(h2_ref[...], wq_ref[...], preferred_element_type=F32).astype(BF16)
    k = PEER_TOPK
    for head in range(PEER_HEADS):
        top = []
        for part, s_ref in enumerate((s1_ref, s2_ref)):
            hp = 2 * head + part
            st = _dot_nt(sk_ref[hp], q[:, hp * LANES:(hp + 1) * LANES]) * LOG2_E
            s_ref[head] = st
            top.append(jnp.concatenate(_top_values(st, k), axis=0))
        sv1, sv2 = top
        cand = [sv1[0:1] + sv2]
        cand += [sv1[a:a + 1] + sv2[0:SUBLANES] for a in range(1, SUBLANES)]
        cand += [sv1[SUBLANES:] + sv2[0:1]]
        best = _top_values(jnp.concatenate(cand, axis=0), k)
        m = best[0]
        zsum = jnp.zeros_like(m)
        for t in best:
            zsum = zsum + jnp.exp2(t - m)
        tau_ref[head:head + 1] = best[k - 1]
        cc_ref[head:head + 1] = m + jnp.log2(zsum)


def _peer_route_call(h2, wq, subkeys, layer, tm):
    tokens, d = h2.shape
    score_spec = pl.BlockSpec((PEER_HEADS, PEER_KEYS, tm), lambda i: (0, 0, i))
    stat_spec = pl.BlockSpec((PEER_HEADS, tm), lambda i: (0, i))
    score_shape = jax.ShapeDtypeStruct((PEER_HEADS, PEER_KEYS, tokens), F32)
    stat_shape = jax.ShapeDtypeStruct((PEER_HEADS, tokens), F32)
    return pl.pallas_call(
        _peer_route_kernel,
        grid=(tokens // tm,),
        in_specs=[pl.BlockSpec((tm, d), lambda i: (i, 0)),
                  pl.BlockSpec((None,) + wq.shape[1:], lambda i: (layer, 0, 0)),
                  pl.BlockSpec((None,) + subkeys.shape[1:], lambda i: (layer, 0, 0, 0))],
        out_specs=[score_spec, score_spec, stat_spec, stat_spec],
        out_shape=[score_shape, score_shape, stat_shape, stat_shape],
        compiler_params=_params(("parallel",)),
        name="peer_route",
    )(h2, wq, subkeys)


def _peer_mix_kernel(h2_ref, ux_ref, uy_ref, vtx_ref, vty_ref, s1x_ref, s1y_ref, s2_ref, tau_ref, cc_ref,
                     x1_ref, g2_ref, lng_ref, lnb_ref, o_ref,
                     acc_ref, stx_ref, sty_ref, wx_ref, wy_ref, *, tt):
    j = pl.program_id(1)

    @pl.when(j == 0)
    def _():
        acc_ref[...] = jnp.zeros(acc_ref.shape, F32)
        sty_ref[...] = jnp.zeros(sty_ref.shape, F32)
        wx_ref[...] = jnp.zeros(wx_ref.shape, BF16)

    per_tile = SUBLANES // 2
    width = 2 * LANES

    def half_step(u_ref, st_out, st_in, s1_ref, row0, w_out, vt_ref, w_in):
        for c in range(tt // width):
            cols = slice(c * width, (c + 1) * width)
            st_out[:, cols] = _dot_nt(u_ref[...], h2_ref[cols, :])
            for half in range(width // LANES):
                lanes = slice(c * width + half * LANES, c * width + (half + 1) * LANES)
                for a in range(per_tile):
                    g = jnp.zeros((PEER_KEYS, LANES), F32)
                    for head in range(PEER_HEADS):
                        zsum = s1_ref[head, row0 + a:row0 + a + 1, lanes] + s2_ref[head, :, lanes]
                        g = g + jnp.where(zsum >= tau_ref[head:head + 1, lanes],
                                          jnp.exp2(zsum - cc_ref[head:head + 1, lanes]), 0.0)
                    rows = slice(a * PEER_KEYS, (a + 1) * PEER_KEYS)
                    w_out[rows, lanes] = (_gelu(st_in[rows, lanes]) * g).astype(BF16)
            acc_ref[:, cols] += jnp.dot(vt_ref[...], w_in[:, cols], preferred_element_type=F32)

    half_step(ux_ref, stx_ref, sty_ref, s1x_ref, per_tile, wy_ref, vtx_ref, wx_ref)
    half_step(uy_ref, sty_ref, stx_ref, s1y_ref, 0, wx_ref, vty_ref, wy_ref)

    @pl.when(j == pl.num_programs(1) - 1)
    def _():
        y = acc_ref[...].T
        o_ref[...] = _layer_norm(ALPHA * x1_ref[...] + g2_ref[...] * y, lng_ref[...], lnb_ref[...])


def _peer_mix_call(h2, u, vt, layer, s1, s2, tau, cc, x1, mod, lng, lnb, seq, tt):
    tokens, d = h2.shape
    bsz = tokens // seq
    tn = (SUBLANES // 2) * PEER_KEYS
    n_tiles = u.shape[1] // tn
    pairs = n_tiles // 2
    per_batch = seq // tt
    assert u.shape[1] % (2 * tn) == 0 and tt % (2 * LANES) == 0
    clamp = lambda t: jnp.clip(t, 0, n_tiles - 1)
    once = pl.Buffered(1)
    u_spec = lambda f: pl.BlockSpec((None, tn, d), lambda i, j: (layer, clamp(f(j)), 0))
    vt_spec = lambda f: pl.BlockSpec((None, d, tn), lambda i, j: (layer, 0, clamp(f(j))))
    row_spec = lambda f: pl.BlockSpec((PEER_HEADS, SUBLANES, tt), lambda i, j: (0, jnp.clip(f(j), 0, pairs - 1), i))
    key_spec = pl.BlockSpec((PEER_HEADS, PEER_KEYS, tt), lambda i, j: (0, 0, i), pipeline_mode=once)
    stat_spec = pl.BlockSpec((PEER_HEADS, tt), lambda i, j: (0, i))
    return pl.pallas_call(
        functools.partial(_peer_mix_kernel, tt=tt),
        grid=(tokens // tt, pairs + 1),
        in_specs=[pl.BlockSpec((tt, d), lambda i, j: (i, 0), pipeline_mode=once),
                  u_spec(lambda j: 2 * j), u_spec(lambda j: 2 * j + 1),
                  vt_spec(lambda j: 2 * j - 2), vt_spec(lambda j: 2 * j - 1),
                  row_spec(lambda j: j - 1), row_spec(lambda j: j),
                  key_spec, stat_spec, stat_spec,
                  pl.BlockSpec((tt, d), lambda i, j: (i, 0), pipeline_mode=once),
                  pl.BlockSpec((None, 1, d), lambda i, j: ((layer * bsz + i // per_batch) * 6 + 5, 0, 0)),
                  pl.BlockSpec((None, 1, d), lambda i, j: (2 * layer + 1, 0, 0)),
                  pl.BlockSpec((None, 1, d), lambda i, j: (2 * layer + 1, 0, 0))],
        out_specs=pl.BlockSpec((tt, d), lambda i, j: (i, 0)),
        out_shape=jax.ShapeDtypeStruct((tokens, d), F32),
        scratch_shapes=[pltpu.VMEM((d, tt), F32),
                        pltpu.VMEM((tn, tt), F32), pltpu.VMEM((tn, tt), F32),
                        pltpu.VMEM((tn, tt), BF16), pltpu.VMEM((tn, tt), BF16)],
        compiler_params=_params(("parallel", "arbitrary")),
        name="peer_mix",
    )(h2, u, u, vt, vt, s1, s1, s2, tau, cc, x1, mod, lng, lnb)


def _rope_tables(seq):
    inv = 1.0 / (ROPE_THETA ** (jnp.arange(0, HEAD_DIM, 2, dtype=F32) / HEAD_DIM))
    ang = jnp.arange(seq, dtype=F32)[:, None] * inv[None, :]
    cos, sin = jnp.cos(ang), jnp.sin(ang)
    zero = jnp.zeros_like(sin)
    reps = LANES // HEAD_DIM
    cos_t = jnp.tile(jnp.concatenate([cos, cos], axis=1), (1, reps))
    sin_lo = jnp.tile(jnp.concatenate([-sin, zero], axis=1), (1, reps))
    sin_hi = jnp.tile(jnp.concatenate([zero, sin], axis=1), (1, reps))
    return cos_t, sin_lo, sin_hi


def _block_diag(w):
    nb, bs, _ = w.shape
    eye = jnp.eye(nb, dtype=w.dtype)
    return (eye[:, None, :, None] * w[:, :, None, :]).reshape(nb * bs, nb * bs)


def _tile(n, want):
    t = min(n, want)
    assert n % t == 0
    return t


def kernel(x, c, w_ada, b_ada, w_in, b_gate, a_sink, b_rpb, c_lambda, c_norm_g, d_conv_w, d_conv_b, d_wa, d_ba, d_wx, d_bx, d_lam, w_branch, w_out, ln_g, ln_b, p_wq, p_subkeys, p_u, p_v):
    bsz, seq, d = x.shape
    tokens = bsz * seq
    rows = seq // GRID_W
    assert d == D_MODEL and seq % (B_QROWS * GRID_W) == 0 and rows // B_QROWS >= 3

    c_pad = jnp.pad(c, ((0, SUBLANES - bsz), (0, 0)))
    mod = _ada_call(c_pad, w_ada, b_ada)[:, :bsz].reshape(DEPTH * bsz * 6, 1, d)
    cos_t, sin_lo, sin_hi = _rope_tables(seq)
    bias_tiles = _nbr_bias_tables(b_rpb, rows)

    w_front = w_in[:, :, :OFF_G].astype(BF16)
    w_gate = w_in[:, :, OFF_G:].astype(BF16)
    b_gate3 = b_gate.reshape(DEPTH, 1, N_BRANCH * d)
    wbr16 = w_branch.astype(BF16)
    wo16 = w_out.astype(BF16)
    ln_g2 = ln_g.reshape(DEPTH * 2, 1, d)
    ln_b2 = ln_b.reshape(DEPTH * 2, 1, d)
    wq16 = p_wq.astype(BF16)
    sk16 = p_subkeys.astype(BF16).reshape(DEPTH, 2 * PEER_HEADS, PEER_KEYS, PEER_QDIM // 2)
    u16 = p_u.astype(BF16)
    vt16 = jnp.swapaxes(p_v, 1, 2).astype(BF16)
    blocks = jax.vmap(jax.vmap(_block_diag))
    lru_w = jnp.concatenate([blocks(d_wa), blocks(d_wx)], axis=-1).astype(BF16)
    lru_b = jnp.concatenate([d_ba, d_bx], axis=-1)
    conv_b = d_conv_b.reshape(DEPTH, 1, D_WIDTH)
    norm_g = c_norm_g.reshape(DEPTH, 1, C_V)

    for l in range(DEPTH):
        aq, akv, bq, bk, bv, cq, ck, cv, dx, dg = _inproj_call(
            x, mod, w_front, l, cos_t, sin_lo, sin_hi, _tile(seq, 512))
        oa = _window_call(a_sink, aq, akv, l)
        ob = _nbr_call(bq, bk, bv, bias_tiles, l)
        oc = _diff_call(c_lambda, norm_g, cq, ck, cv, l, _tile(seq, DIFF_TQ), _tile(seq, DIFF_TK))
        hf, hb = _lru_call(dx, d_conv_w, conv_b, lru_w, lru_b, d_lam, l, _tile(seq, 256))
        x1, h2 = _merge_call(x, mod, oa, ob, oc, hf, hb, dg, w_gate, b_gate3, wbr16, wo16,
                             ln_g2, ln_b2, l, _tile(seq, 256))
        h2f = h2.reshape(tokens, d)
        s1, s2, tau, cc = _peer_route_call(h2f, wq16, sk16, l, _tile(tokens, 512))
        xn = _peer_mix_call(h2f, u16, vt16, l, s1, s2, tau, cc, x1.reshape(tokens, d), mod,
                            ln_g2, ln_b2, seq, _tile(seq, 1024))
        x = xn.reshape(bsz, seq, d)
    return x
```

```python
import functools
import math

import numpy as np
import jax
import jax.numpy as jnp
from jax import lax
from jax.experimental import pallas as pl
from jax.experimental.pallas import tpu as pltpu

F32 = jnp.float32
BF16 = jnp.bfloat16

LANES = 128
SUBLANES = 8
VMEM_LIMIT = 56 * 1024 * 1024

D_MODEL = 1024
DEPTH = 4
GRID_W = 64
HEAD_DIM = 64
ROPE_THETA = 10000.0
NEG_INF = -1e30
LN_EPS = 1e-5
A_HEADS = 8
A_KV_HEADS = 2
A_WINDOW = 128
A_BLOCK = 128
B_HEADS = 8
B_KH = 8
B_KW = 16
C_HEADS = 4
C_VDIM = 2 * HEAD_DIM
D_WIDTH = 512
D_BLOCKS = 8
D_CONV = 4
LRU_C = 8.0
N_BRANCH = 4
BRANCH_W = 512
PEER_HEADS = 8
PEER_KEYS = 128
PEER_N = PEER_KEYS * PEER_KEYS
PEER_QDIM = 256
PEER_TOPK = 16
ALPHA = (2.0 * DEPTH) ** 0.25
LOG2_E = math.log2(math.e)

A_Q = A_HEADS * HEAD_DIM
A_KV = A_KV_HEADS * HEAD_DIM
B_QKV = B_HEADS * HEAD_DIM
C_QK = C_HEADS * 2 * HEAD_DIM
C_V = C_HEADS * C_VDIM
OFF_A = 0
OFF_B = A_Q + 2 * A_KV
OFF_C = OFF_B + 3 * B_QKV
OFF_D = OFF_C + 2 * C_QK + C_V
OFF_G = OFF_D + 2 * D_WIDTH

DIFF_TQ, DIFF_TK = 1024, 2048
A_QBLOCKS = 2

B_QROWS = 4
B_KROWS = B_QROWS + B_KH


def _params(sem):
    return pltpu.CompilerParams(dimension_semantics=sem, vmem_limit_bytes=VMEM_LIMIT)


def _gelu(x):
    c = math.sqrt(2.0 / math.pi)
    half = 0.5 * x
    return half + half * jnp.tanh(x * (c + (c * 0.044715) * (x * x)))


def _layer_norm(y, g, b):
    mu = jnp.mean(y, axis=-1, keepdims=True)
    yc = y - mu
    var = jnp.mean(yc * yc, axis=-1, keepdims=True)
    return yc * lax.rsqrt(var + LN_EPS) * g + b


def _dot_nt(a, b):
    return lax.dot_general(a, b, (((1,), (1,)), ((), ())), preferred_element_type=F32)


def _low_half_mask():
    return lax.broadcasted_iota(jnp.int32, (1, LANES), 1) < HEAD_DIM


def _ada_kernel(c_ref, w_ref, b_ref, o_ref):
    c = c_ref[...]
    ca = c * jax.nn.sigmoid(c)
    o_ref[0] = jnp.dot(ca, w_ref[0], precision=lax.Precision.HIGHEST,
                       preferred_element_type=F32) + b_ref[0]


def _ada_call(c_pad, w_ada, b_ada):
    depth, d, cols = w_ada.shape
    rows = c_pad.shape[0]
    tn = 1024
    return pl.pallas_call(
        _ada_kernel,
        grid=(depth, cols // tn),
        in_specs=[pl.BlockSpec((rows, d), lambda l, j: (0, 0)),
                  pl.BlockSpec((1, d, tn), lambda l, j: (l, 0, j)),
                  pl.BlockSpec((1, 1, tn), lambda l, j: (l, 0, j))],
        out_specs=pl.BlockSpec((1, rows, tn), lambda l, j: (l, 0, j)),
        out_shape=jax.ShapeDtypeStruct((depth, rows, cols), F32),
        compiler_params=_params(("arbitrary", "arbitrary")),
        name="adaln_mod",
    )(c_pad, w_ada, b_ada.reshape(depth, 1, cols))


def _inproj_kernel(x_ref, sh_ref, sc_ref, w_ref, cos_ref, slo_ref, shi_ref,
                   aq_ref, akv_ref, bq_ref, bk_ref, bv_ref, cq_ref, ck_ref, cv_ref,
                   dx_ref, dg_ref):
    h = (x_ref[0] * (1.0 + sc_ref[...]) + sh_ref[...]).astype(BF16)
    cos = cos_ref[...]
    slo = slo_ref[...]
    shi = shi_ref[...]
    qscale = HEAD_DIM ** -0.5

    def rope(v):
        return (v * cos + pltpu.roll(v, LANES - HEAD_DIM // 2, 1) * slo
                + pltpu.roll(v, HEAD_DIM // 2, 1) * shi)

    def tiles(y, start, n):
        return [y[:, (start + i) * LANES:(start + i + 1) * LANES] for i in range(n)]

    ya = jnp.dot(h, w_ref[:, OFF_A:OFF_B], preferred_element_type=F32)
    for i, t in enumerate(tiles(ya, 0, A_Q // LANES)):
        aq_ref[0, :, i * LANES:(i + 1) * LANES] = (rope(t) * qscale).astype(BF16)
    ak = rope(ya[:, A_Q:A_Q + LANES])
    av = ya[:, A_Q + LANES:A_Q + 2 * LANES]
    for i, t in enumerate((ak, av, pltpu.roll(ak, HEAD_DIM, 1), pltpu.roll(av, HEAD_DIM, 1))):
        akv_ref[0, :, i * LANES:(i + 1) * LANES] = t.astype(BF16)

    yb = jnp.dot(h, w_ref[:, OFF_B:OFF_C], preferred_element_type=F32)
    bq_ref[0] = (yb[:, 0:B_QKV] * qscale).astype(BF16)
    bk_ref[0] = yb[:, B_QKV:2 * B_QKV].astype(BF16)
    bv_ref[0] = yb[:, 2 * B_QKV:3 * B_QKV].astype(BF16)

    yc = jnp.dot(h, w_ref[:, OFF_C:OFF_D], preferred_element_type=F32)
    nt = C_QK // LANES
    for i, t in enumerate(tiles(yc, 0, nt)):
        cq_ref[0, :, i * LANES:(i + 1) * LANES] = (rope(t) * qscale).astype(BF16)
    for i, t in enumerate(tiles(yc, nt, nt)):
        ck_ref[0, :, i * LANES:(i + 1) * LANES] = rope(t).astype(BF16)
    cv_ref[0] = yc[:, 2 * C_QK:2 * C_QK + C_V].astype(BF16)

    yd = jnp.dot(h, w_ref[:, OFF_D:OFF_G], preferred_element_type=F32)
    dx_ref[0] = yd[:, 0:D_WIDTH]
    dg_ref[0] = yd[:, D_WIDTH:2 * D_WIDTH]


def _mod_spec(layer, which, bsz, d):
    return pl.BlockSpec((None, 1, d), lambda b, *_: ((layer * bsz + b) * 6 + which, 0, 0))


def _inproj_call(x, mod, w_front, layer, cos_t, sin_lo, sin_hi, tm):
    bsz, seq, d = x.shape
    tok = lambda w: pl.BlockSpec((1, tm, w), lambda b, i: (b, i, 0))
    tab = pl.BlockSpec((tm, LANES), lambda b, i: (i, 0))
    outs = [(A_Q, BF16), (4 * LANES, BF16), (B_QKV, BF16), (B_QKV, BF16), (B_QKV, BF16),
            (C_QK, BF16), (C_QK, BF16), (C_V, BF16), (D_WIDTH, F32), (D_WIDTH, F32)]
    return pl.pallas_call(
        _inproj_kernel,
        grid=(bsz, seq // tm),
        in_specs=[tok(d), _mod_spec(layer, 0, bsz, d), _mod_spec(layer, 1, bsz, d),
                  pl.BlockSpec((None,) + w_front.shape[1:], lambda b, i: (layer, 0, 0)), tab, tab, tab],
        out_specs=[tok(w) for w, _ in outs],
        out_shape=[jax.ShapeDtypeStruct((bsz, seq, w), dt) for w, dt in outs],
        compiler_params=_params(("parallel", "arbitrary")),
        name="in_proj",
    )(x, mod, mod, w_front, cos_t, sin_lo, sin_hi)


def _window_kernel(sink_ref, q_ref, *rest, seq, layer):
    kv_refs, o_ref = rest[:-1], rest[-1]
    n = pl.program_id(1)
    blk = A_BLOCK * A_QBLOCKS
    nk = A_BLOCK * len(kv_refs)
    kv = jnp.concatenate([r[0] for r in kv_refs], axis=0)
    k_tiles = (kv[:, 0:LANES], kv[:, 2 * LANES:3 * LANES])
    v_tiles = (kv[:, LANES:2 * LANES], kv[:, 3 * LANES:4 * LANES])
    qi = lax.broadcasted_iota(jnp.int32, (2 * blk, nk), 0) % blk
    kj = lax.broadcasted_iota(jnp.int32, (2 * blk, nk), 1)
    rel = kj - A_BLOCK - qi
    kpos = n * blk + kj - A_BLOCK
    valid = (jnp.abs(rel) <= A_WINDOW) & (kpos >= 0) & (kpos < seq)
    low_rows = lax.broadcasted_iota(jnp.int32, (2 * blk, 1), 0) < blk
    low = _low_half_mask()
    heads_per_kv = A_HEADS // A_KV_HEADS
    for pair in range(A_HEADS // 2):
        kv_head = (2 * pair) // heads_per_kv
        lo_src, hi_src = kv_head, 1 - kv_head
        qp = q_ref[0, :, pair * LANES:(pair + 1) * LANES]
        zero = jnp.zeros_like(qp)
        s = jnp.concatenate([_dot_nt(jnp.where(low, qp, zero), k_tiles[lo_src]),
                             _dot_nt(jnp.where(low, zero, qp), k_tiles[hi_src])], axis=0)
        s = jnp.where(valid, s, NEG_INF)
        sk = jnp.where(low_rows, sink_ref[layer, 2 * pair], sink_ref[layer, 2 * pair + 1])
        m = jnp.maximum(jnp.max(s, axis=-1, keepdims=True), sk)
        p = jnp.exp(s - m)
        p = (p / (jnp.sum(p, axis=-1, keepdims=True) + jnp.exp(sk - m))).astype(BF16)
        o_lo = jnp.dot(p[0:blk], v_tiles[lo_src], preferred_element_type=F32)
        o_hi = jnp.dot(p[blk:], v_tiles[hi_src], preferred_element_type=F32)
        o_ref[0, :, pair * LANES:(pair + 1) * LANES] = jnp.where(low, o_lo, o_hi).astype(BF16)


def _window_call(sink, aq, akv, layer):
    bsz, seq, _ = aq.shape
    nb = seq // A_BLOCK
    nq = A_BLOCK * A_QBLOCKS
    assert seq % nq == 0
    kvspec = lambda i: pl.BlockSpec((1, A_BLOCK, 4 * LANES),
                                    lambda b, n: (b, jnp.clip(n * A_QBLOCKS - 1 + i, 0, nb - 1), 0))
    nkv = A_QBLOCKS + 2
    return pl.pallas_call(
        functools.partial(_window_kernel, seq=seq, layer=layer),
        grid=(bsz, seq // nq),
        in_specs=[pl.BlockSpec(memory_space=pltpu.SMEM),
                  pl.BlockSpec((1, nq, A_Q), lambda b, n: (b, n, 0))] + [kvspec(i) for i in range(nkv)],
        out_specs=pl.BlockSpec((1, nq, A_Q), lambda b, n: (b, n, 0)),
        out_shape=jax.ShapeDtypeStruct((bsz, seq, A_Q), BF16),
        compiler_params=_params(("parallel", "arbitrary")),
        name="window_gqa",
    )(sink, aq, *([akv] * nkv))


def _nbr_bias_tables(rpb, rows):
    groups = rows // B_QROWS
    qr = np.arange(B_QROWS)
    kr = np.arange(B_KROWS)
    row_onehot = np.zeros((3, B_QROWS, B_KROWS, 2 * B_KH - 1), np.float32)
    row_valid = np.zeros((3, B_QROWS, B_KROWS), bool)
    for var, g in enumerate((0, 1, groups - 1)):
        base = int(np.clip(g - 1, 0, groups - 3)) * B_QROWS
        r = g * B_QROWS + qr
        rstart = np.clip(r - B_KH // 2, 0, rows - B_KH)
        krow = base + kr
        ok = (krow[None, :] >= rstart[:, None]) & (krow[None, :] < rstart[:, None] + B_KH)
        drow = np.clip(krow[None, :] - r[:, None] + (B_KH - 1), 0, 2 * B_KH - 2)
        row_valid[var] = ok
        row_onehot[var, qr[:, None], kr[None, :], drow] = 1.0
    qc = np.arange(GRID_W)
    kc = np.arange(GRID_W)
    cstart = np.clip(qc - B_KW // 2, 0, GRID_W - B_KW)
    col_valid = (kc[None, :] >= cstart[:, None]) & (kc[None, :] < cstart[:, None] + B_KW)
    dcol = np.clip(kc[None, :] - qc[:, None], 1 - B_KW, B_KW - 1) + (B_KW - 1)
    col_onehot = np.zeros((GRID_W, GRID_W, 2 * B_KW - 1), np.float32)
    col_onehot[qc[:, None], kc[None, :], dcol] = 1.0
    valid = row_valid[:, :, None, :, None] & col_valid[None, None, :, None, :]
    hi = lax.Precision.HIGHEST
    by_col = jnp.einsum('lhab,qkb->lhaqk', rpb, jnp.asarray(col_onehot), precision=hi)
    tiles = jnp.einsum('lhaqk,vrsa->lvhrqsk', by_col, jnp.asarray(row_onehot), precision=hi)
    tiles = jnp.where(jnp.asarray(valid)[None, :, None], tiles, NEG_INF)
    depth = rpb.shape[0]
    return tiles.reshape(depth, 3, B_HEADS, B_QROWS * GRID_W, B_KROWS * GRID_W)


def _nbr_kernel(q_ref, k0_ref, k1_ref, k2_ref, v0_ref, v1_ref, v2_ref, bias_ref, o_ref):
    nq = B_QROWS * GRID_W
    k = jnp.concatenate([k0_ref[0], k1_ref[0], k2_ref[0]], axis=0)
    v = jnp.concatenate([v0_ref[0], v1_ref[0], v2_ref[0]], axis=0)
    low = _low_half_mask()
    for pair in range(B_HEADS // 2):
        cols = slice(pair * LANES, (pair + 1) * LANES)
        qp = q_ref[0, :, cols]
        zero = jnp.zeros_like(qp)
        q2 = jnp.concatenate([jnp.where(low, qp, zero), jnp.where(low, zero, qp)], axis=0)
        bias = jnp.concatenate([bias_ref[0, 0, 2 * pair], bias_ref[0, 0, 2 * pair + 1]], axis=0)
        s = _dot_nt(q2, k[:, cols]) + bias
        m = jnp.max(s, axis=-1, keepdims=True)
        p = jnp.exp(s - m)
        p = p / jnp.sum(p, axis=-1, keepdims=True)
        o2 = jnp.dot(p.astype(BF16), v[:, cols], preferred_element_type=F32)
        o_ref[0, :, cols] = jnp.where(low, o2[0:nq], o2[nq:]).astype(BF16)


def _nbr_call(bq, bk, bv, bias_tiles, layer):
    bsz, seq, w = bq.shape
    nq = B_QROWS * GRID_W
    groups = seq // nq
    start = lambda g: jnp.clip(g - 1, 0, groups - 3)
    kvspec = lambda off: pl.BlockSpec((1, nq, w), lambda b, g: (b, start(g) + off, 0))
    variant = lambda g: jnp.where(g == 0, 0, jnp.where(g == groups - 1, 2, 1))
    return pl.pallas_call(
        _nbr_kernel,
        grid=(bsz, groups),
        in_specs=[pl.BlockSpec((1, nq, w), lambda b, g: (b, g, 0)),
                  kvspec(0), kvspec(1), kvspec(2), kvspec(0), kvspec(1), kvspec(2),
                  pl.BlockSpec((1, 1, B_HEADS, nq, B_KROWS * GRID_W),
                               lambda b, g: (layer, variant(g), 0, 0, 0))],
        out_specs=pl.BlockSpec((1, nq, w), lambda b, g: (b, g, 0)),
        out_shape=jax.ShapeDtypeStruct((bsz, seq, w), BF16),
        compiler_params=_params(("parallel", "arbitrary")),
        name="neighbourhood_attn",
    )(bq, bk, bk, bk, bv, bv, bv, bias_tiles)


def _diff_kernel(lam_ref, ng_ref, q_ref, k_ref, v_ref, o_ref, m_ref, acc_ref, *, lam_init):
    j = pl.program_id(3)
    tq = q_ref.shape[1]
    tk = k_ref.shape[1]

    @pl.when(j == 0)
    def _():
        m_ref[...] = jnp.full(m_ref.shape, -jnp.inf, F32)
        acc_ref[...] = jnp.zeros(acc_ref.shape, F32)

    low = _low_half_mask()
    q = q_ref[0]
    k = k_ref[0]
    zero = jnp.zeros_like(q)
    v_ext = jnp.concatenate([v_ref[0], jnp.ones((tk, LANES), BF16)], axis=1)
    scores = [_dot_nt(qm, k) for qm in (jnp.where(low, q, zero), jnp.where(low, zero, q))]
    for mp, s in enumerate(scores):
        tiles = [s[:, c * LANES:(c + 1) * LANES] for c in range(tk // LANES)]
        part = tiles[0]
        for t in tiles[1:]:
            part = jnp.maximum(part, t)
        m_old = m_ref[mp]
        m_new = jnp.maximum(m_old, jnp.broadcast_to(jnp.max(part, axis=-1, keepdims=True), (tq, LANES)))
        alpha = jnp.exp(m_old - m_new)
        p = jnp.concatenate([jnp.exp(t - m_new) for t in tiles], axis=1).astype(BF16)
        acc_ref[mp] = (jnp.concatenate([alpha, alpha], axis=1) * acc_ref[mp]
                       + jnp.dot(p, v_ext, preferred_element_type=F32))
        m_ref[mp] = m_new

    @pl.when(j == pl.num_programs(3) - 1)
    def _():
        lf = lam_ref[...]
        lam = (jnp.exp(jnp.sum(lf[0:1] * lf[1:2], axis=-1, keepdims=True))
               - jnp.exp(jnp.sum(lf[2:3] * lf[3:4], axis=-1, keepdims=True)) + lam_init)
        a1 = acc_ref[0]
        a2 = acc_ref[1]
        o = a1[:, 0:C_VDIM] / a1[:, C_VDIM:] - lam * (a2[:, 0:C_VDIM] / a2[:, C_VDIM:])
        o = o * lax.rsqrt(jnp.mean(o * o, axis=-1, keepdims=True) + LN_EPS)
        o_ref[0] = (o * ng_ref[...] * (1.0 - lam_init)).astype(BF16)


def _diff_call(lam, norm_g, cq, ck, cv, layer, tq, tk):
    bsz, seq, _ = cq.shape
    lam_init = 0.8 - 0.6 * math.exp(-0.3 * layer)
    return pl.pallas_call(
        functools.partial(_diff_kernel, lam_init=lam_init),
        grid=(bsz, C_HEADS, seq // tq, seq // tk),
        in_specs=[pl.BlockSpec((None, 4, HEAD_DIM), lambda b, h, i, j: (layer, 0, 0)),
                  pl.BlockSpec((None, 1, C_VDIM), lambda b, h, i, j: (layer, 0, h)),
                  pl.BlockSpec((1, tq, LANES), lambda b, h, i, j: (b, i, h)),
                  pl.BlockSpec((1, tk, LANES), lambda b, h, i, j: (b, j, h)),
                  pl.BlockSpec((1, tk, C_VDIM), lambda b, h, i, j: (b, j, h))],
        out_specs=pl.BlockSpec((1, tq, C_VDIM), lambda b, h, i, j: (b, i, h)),
        out_shape=jax.ShapeDtypeStruct((bsz, seq, C_V), BF16),
        scratch_shapes=[pltpu.VMEM((2, tq, LANES), F32), pltpu.VMEM((2, tq, C_VDIM + LANES), F32)],
        compiler_params=_params(("parallel", "parallel", "parallel", "arbitrary")),
        name="diff_attn",
    )(lam, norm_g, cq, ck, cv)


def _lru_kernel(xf_ref, xfp_ref, xfn_ref, xb_ref, xbp_ref, xbn_ref, cw_ref, cb_ref,
                w_ref, bias_ref, lam_ref, hf_ref, hb_ref, ext_ref, carry_ref, *, tc):
    j = pl.program_id(1)
    nc = pl.num_programs(1)
    halo = 8
    left = D_CONV // 2

    @pl.when(j == 0)
    def _():
        carry_ref[...] = jnp.zeros(carry_ref.shape, F32)

    z = -lam_ref[...]
    softplus = jnp.maximum(z, 0.0) + jnp.log1p(jnp.exp(-jnp.abs(z)))
    rows = lax.broadcasted_iota(jnp.int32, (tc, 1), 0)

    def gates(d, cur_ref, prev_ref, next_ref, chunk):
        ext_ref[0:halo] = jnp.where(chunk > 0, prev_ref[0], 0.0)
        ext_ref[halo:halo + tc] = cur_ref[0]
        ext_ref[halo + tc:2 * halo + tc] = jnp.where(chunk < nc - 1, next_ref[0], 0.0)
        xc = cb_ref[...]
        for t in range(D_CONV):
            xc = xc + cw_ref[t:t + 1] * ext_ref[pl.ds(halo - left + t, tc), :]
        pre = jnp.dot(xc.astype(BF16), w_ref[d], preferred_element_type=F32) + bias_ref[d:d + 1]
        r = jax.nn.sigmoid(pre[:, 0:D_WIDTH])
        i = jax.nn.sigmoid(pre[:, D_WIDTH:2 * D_WIDTH])
        log_a = (-LRU_C) * r * softplus[d:d + 1]
        a = jnp.exp(log_a)
        b = jnp.sqrt(-jnp.tanh(log_a) * (a * a + 1.0)) * (i * xc)
        return a, b

    a, b = gates(0, xf_ref, xfp_ref, xfn_ref, j)
    step = 1
    while step < tc:
        keep = rows >= step
        a_sh = jnp.where(keep, pltpu.roll(a, step, 0), 1.0)
        b_sh = jnp.where(keep, pltpu.roll(b, step, 0), 0.0)
        b = a * b_sh + b
        a = a * a_sh
        step *= 2
    h = b + a * carry_ref[0:1]
    hf_ref[0] = h
    carry_ref[0:1] = h[tc - 1:tc]

    a, b = gates(1, xb_ref, xbp_ref, xbn_ref, nc - 1 - j)
    step = 1
    while step < tc:
        keep = rows < tc - step
        a_sh = jnp.where(keep, pltpu.roll(a, tc - step, 0), 1.0)
        b_sh = jnp.where(keep, pltpu.roll(b, tc - step, 0), 0.0)
        b = a * b_sh + b
        a = a * a_sh
        step *= 2
    h = b + a * carry_ref[1:2]
    hb_ref[0] = h
    carry_ref[1:2] = h[0:1]


def _lru_call(dx, conv_w, conv_b, w_gates, b_gates, lam, layer, tc):
    bsz, seq, w = dx.shape
    nc = seq // tc
    hb = tc // 8
    nh = seq // 8
    cur = lambda f: pl.BlockSpec((1, tc, w), f)
    halo = lambda f: pl.BlockSpec((1, 8, w), f)
    full = lambda a: pl.BlockSpec((None,) + a.shape[1:], lambda b, j: (layer,) + (0,) * (a.ndim - 1))
    mirror = lambda j: nc - 1 - j
    return pl.pallas_call(
        functools.partial(_lru_kernel, tc=tc),
        grid=(bsz, nc),
        in_specs=[cur(lambda b, j: (b, j, 0)),
                  halo(lambda b, j: (b, jnp.maximum(j * hb - 1, 0), 0)),
                  halo(lambda b, j: (b, jnp.minimum((j + 1) * hb, nh - 1), 0)),
                  cur(lambda b, j: (b, mirror(j), 0)),
                  halo(lambda b, j: (b, jnp.maximum(mirror(j) * hb - 1, 0), 0)),
                  halo(lambda b, j: (b, jnp.minimum((mirror(j) + 1) * hb, nh - 1), 0)),
                  full(conv_w), full(conv_b), full(w_gates), full(b_gates), full(lam)],
        out_specs=[cur(lambda b, j: (b, j, 0)), cur(lambda b, j: (b, mirror(j), 0))],
        out_shape=[jax.ShapeDtypeStruct((bsz, seq, w), F32)] * 2,
        scratch_shapes=[pltpu.VMEM((tc + 16, w), F32), pltpu.VMEM((2, w), F32)],
        compiler_params=_params(("parallel", "arbitrary")),
        name="rg_lru",
    )(dx, dx, dx, dx, dx, dx, conv_w, conv_b, w_gates, b_gates, lam)


def _merge_kernel(x_ref, sh1_ref, sc1_ref, g1_ref, sh2_ref, sc2_ref,
                  oa_ref, ob_ref, oc_ref, hf_ref, hb_ref, dg_ref,
                  wg_ref, bg_ref, wbr_ref, wo_ref, lng_ref, lnb_ref,
                  x1_ref, h2_ref):
    x = x_ref[0]
    h = (x * (1.0 + sc1_ref[...]) + sh1_ref[...]).astype(BF16)
    od = ((hf_ref[0] + hb_ref[0]) * _gelu(dg_ref[0])).astype(BF16)
    branches = (oa_ref[0], ob_ref[0], oc_ref[0], od)
    mixed = None
    for n in range(N_BRANCH):
        cols = slice(n * D_MODEL, (n + 1) * D_MODEL)
        gate = jax.nn.sigmoid(jnp.dot(h, wg_ref[:, cols], preferred_element_type=F32) + bg_ref[:, cols])
        term = gate * jnp.dot(branches[n], wbr_ref[n], preferred_element_type=F32)
        mixed = term if mixed is None else mixed + term
    mix = jnp.dot(mixed.astype(BF16), wo_ref[...], preferred_element_type=F32)
    x1 = _layer_norm(ALPHA * x + g1_ref[...] * mix, lng_ref[...], lnb_ref[...])
    x1_ref[0] = x1
    h2_ref[0] = (x1 * (1.0 + sc2_ref[...]) + sh2_ref[...]).astype(BF16)


def _merge_call(x, mod, oa, ob, oc, hf, hb, dg, wg, bg, wbr, wo, lng, lnb, layer, tm):
    bsz, seq, d = x.shape
    tok = lambda w: pl.BlockSpec((1, tm, w), lambda b, i: (b, i, 0))
    full = lambda a: pl.BlockSpec((None,) + a.shape[1:], lambda b, i: (layer,) + (0,) * (a.ndim - 1))
    ln = pl.BlockSpec((None, 1, d), lambda b, i: (2 * layer, 0, 0))
    return pl.pallas_call(
        _merge_kernel,
        grid=(bsz, seq // tm),
        in_specs=[tok(d)] + [_mod_spec(layer, which, bsz, d) for which in range(5)] + [tok(BRANCH_W)] * 6
                 + [full(wg), full(bg), full(wbr), full(wo), ln, ln],
        out_specs=[tok(d), tok(d)],
        out_shape=[jax.ShapeDtypeStruct((bsz, seq, d), F32),
                   jax.ShapeDtypeStruct((bsz, seq, d), BF16)],
        compiler_params=_params(("parallel", "arbitrary")),
        name="merge_out_ln",
    )(x, mod, mod, mod, mod, mod, oa, ob, oc, hf, hb, dg, wg, bg, wbr, wo, lng, lnb)


def _top_values(work, count):
    vals = []
    for _ in range(count):
        mx = jnp.max(work, axis=0, keepdims=True)
        vals.append(mx)
        work = jnp.where(work == mx, -jnp.inf, work)
    return vals


def _peer_route_kernel(h2_ref, wq_ref, sk_ref, s1_ref, s2_ref, tau_ref):
    q = jnp.dot(h2_ref[...], wq_ref[...], preferred_element_type=F32).astype(BF16)
    k = PEER_TOPK
    pad = jnp.full((SUBLANES - 1, q.shape[0]), -jnp.inf, F32)
    for head in range(PEER_HEADS):
        scores = []
        top = []
        for part in range(2):
            hp = 2 * head + part
            st = _dot_nt(sk_ref[hp], q[:, hp * LANES:(hp + 1) * LANES]) * LOG2_E
            scores.append(st)
            top.append(jnp.concatenate(_top_values(st, k + 1) + [pad], axis=0))
        sv1, sv2 = top
        cand = [sv1[0:1] + sv2]
        cand += [sv1[a:a + 1] + sv2[0:SUBLANES] for a in range(1, SUBLANES)]
        cand += [sv1[SUBLANES:] + sv2[0:1]]
        best = _top_values(jnp.concatenate(cand, axis=0), k + 1)
        m = best[0]
        zsum = jnp.zeros_like(m)
        for t in best[:k]:
            zsum = zsum + jnp.exp2(t - m)
        c = m + jnp.log2(zsum)
        s1_ref[head] = scores[0] - c
        s2_ref[head] = scores[1]
        tau_ref[head:head + 1] = 0.5 * (best[k - 1] + best[k]) - c


def _peer_route_call(h2, wq, subkeys, layer, tm):
    tokens, d = h2.shape
    score_spec = pl.BlockSpec((PEER_HEADS, PEER_KEYS, tm), lambda i: (0, 0, i))
    stat_spec = pl.BlockSpec((PEER_HEADS, tm), lambda i: (0, i))
    score_shape = jax.ShapeDtypeStruct((PEER_HEADS, PEER_KEYS, tokens), F32)
    stat_shape = jax.ShapeDtypeStruct((PEER_HEADS, tokens), F32)
    return pl.pallas_call(
        _peer_route_kernel,
        grid=(tokens // tm,),
        in_specs=[pl.BlockSpec((tm, d), lambda i: (i, 0)),
                  pl.BlockSpec((None,) + wq.shape[1:], lambda i: (layer, 0, 0)),
                  pl.BlockSpec((None,) + subkeys.shape[1:], lambda i: (layer, 0, 0, 0))],
        out_specs=[score_spec, score_spec, stat_spec],
        out_shape=[score_shape, score_shape, stat_shape],
        compiler_params=_params(("parallel",)),
        name="peer_route",
    )(h2, wq, subkeys)


def _peer_mix_kernel(h2_ref, ux_ref, uy_ref, vtx_ref, vty_ref, s1x_ref, s1y_ref, s2_ref, tau_ref,
                     x1_ref, g2_ref, lng_ref, lnb_ref, o_ref,
                     acc_ref, stx_ref, sty_ref, wx_ref, wy_ref, *, tt):
    j = pl.program_id(1)

    @pl.when(j == 0)
    def _():
        acc_ref[...] = jnp.zeros(acc_ref.shape, F32)
        sty_ref[...] = jnp.zeros(sty_ref.shape, F32)
        wx_ref[...] = jnp.zeros(wx_ref.shape, BF16)

    per_tile = SUBLANES // 2
    width = 2 * LANES

    def half_step(u_ref, st_out, st_in, s1_ref, row0, w_out, vt_ref, w_in):
        for c in range(tt // width):
            cols = slice(c * width, (c + 1) * width)
            st_out[:, cols] = _dot_nt(u_ref[...], h2_ref[cols, :])
            for half in range(width // LANES):
                lanes = slice(c * width + half * LANES, c * width + (half + 1) * LANES)
                for a in range(per_tile):
                    g = jnp.zeros((PEER_KEYS, LANES), F32)
                    for head in range(PEER_HEADS):
                        zsum = s1_ref[head, row0 + a:row0 + a + 1, lanes] + s2_ref[head, :, lanes]
                        g = g + jnp.where(zsum >= tau_ref[head:head + 1, lanes], jnp.exp2(zsum), 0.0)
                    rows = slice(a * PEER_KEYS, (a + 1) * PEER_KEYS)
                    w_out[rows, lanes] = (_gelu(st_in[rows, lanes]) * g).astype(BF16)
            acc_ref[:, cols] += jnp.dot(vt_ref[...], w_in[:, cols], preferred_element_type=F32)

    half_step(ux_ref, stx_ref, sty_ref, s1x_ref, per_tile, wy_ref, vtx_ref, wx_ref)
    half_step(uy_ref, sty_ref, stx_ref, s1y_ref, 0, wx_ref, vty_ref, wy_ref)

    @pl.when(j == pl.num_programs(1) - 1)
    def _():
        y = acc_ref[...].T
        o_ref[...] = _layer_norm(ALPHA * x1_ref[...] + g2_ref[...] * y, lng_ref[...], lnb_ref[...])


def _peer_mix_call(h2, u, vt, layer, s1, s2, tau, x1, mod, lng, lnb, seq, tt):
    tokens, d = h2.shape
    bsz = tokens // seq
    tn = (SUBLANES // 2) * PEER_KEYS
    n_tiles = u.shape[1] // tn
    pairs = n_tiles // 2
    per_batch = seq // tt
    assert u.shape[1] % (2 * tn) == 0 and tt % (2 * LANES) == 0
    clamp = lambda t: jnp.clip(t, 0, n_tiles - 1)
    u_spec = lambda f: pl.BlockSpec((None, tn, d), lambda i, j: (layer, clamp(f(j)), 0))
    vt_spec = lambda f: pl.BlockSpec((None, d, tn), lambda i, j: (layer, 0, clamp(f(j))))
    row_spec = lambda f: pl.BlockSpec((PEER_HEADS, SUBLANES, tt), lambda i, j: (0, jnp.clip(f(j), 0, pairs - 1), i))
    key_spec = pl.BlockSpec((PEER_HEADS, PEER_KEYS, tt), lambda i, j: (0, 0, i))
    stat_spec = pl.BlockSpec((PEER_HEADS, tt), lambda i, j: (0, i))
    return pl.pallas_call(
        functools.partial(_peer_mix_kernel, tt=tt),
        grid=(tokens // tt, pairs + 1),
        in_specs=[pl.BlockSpec((tt, d), lambda i, j: (i, 0)),
                  u_spec(lambda j: 2 * j), u_spec(lambda j: 2 * j + 1),
                  vt_spec(lambda j: 2 * j - 2), vt_spec(lambda j: 2 * j - 1),
                  row_spec(lambda j: j - 1), row_spec(lambda j: j),
                  key_spec, stat_spec,
                  pl.BlockSpec((tt, d), lambda i, j: (i, 0)),
                  pl.BlockSpec((None, 1, d), lambda i, j: ((layer * bsz + i // per_batch) * 6 + 5, 0, 0)),
                  pl.BlockSpec((None, 1, d), lambda i, j: (2 * layer + 1, 0, 0)),
                  pl.BlockSpec((None, 1, d), lambda i, j: (2 * layer + 1, 0, 0))],
        out_specs=pl.BlockSpec((tt, d), lambda i, j: (i, 0)),
        out_shape=jax.ShapeDtypeStruct((tokens, d), F32),
        scratch_shapes=[pltpu.VMEM((d, tt), F32),
                        pltpu.VMEM((tn, tt), F32), pltpu.VMEM((tn, tt), F32),
                        pltpu.VMEM((tn, tt), BF16), pltpu.VMEM((tn, tt), BF16)],
        compiler_params=_params(("parallel", "arbitrary")),
        name="peer_mix",
    )(h2, u, u, vt, vt, s1, s1, s2, tau, x1, mod, lng, lnb)


def _rope_tables(seq):
    inv = 1.0 / (ROPE_THETA ** (jnp.arange(0, HEAD_DIM, 2, dtype=F32) / HEAD_DIM))
    ang = jnp.arange(seq, dtype=F32)[:, None] * inv[None, :]
    cos, sin = jnp.cos(ang), jnp.sin(ang)
    zero = jnp.zeros_like(sin)
    reps = LANES // HEAD_DIM
    cos_t = jnp.tile(jnp.concatenate([cos, cos], axis=1), (1, reps))
    sin_lo = jnp.tile(jnp.concatenate([-sin, zero], axis=1), (1, reps))
    sin_hi = jnp.tile(jnp.concatenate([zero, sin], axis=1), (1, reps))
    return cos_t, sin_lo, sin_hi


def _block_diag(w):
    nb, bs, _ = w.shape
    eye = jnp.eye(nb, dtype=w.dtype)
    return (eye[:, None, :, None] * w[:, :, None, :]).reshape(nb * bs, nb * bs)


def _tile(n, want):
    t = min(n, want)
    assert n % t == 0
    return t


def kernel(x, c, w_ada, b_ada, w_in, b_gate, a_sink, b_rpb, c_lambda, c_norm_g, d_conv_w, d_conv_b, d_wa, d_ba, d_wx, d_bx, d_lam, w_branch, w_out, ln_g, ln_b, p_wq, p_subkeys, p_u, p_v):
    bsz, seq, d = x.shape
    tokens = bsz * seq
    rows = seq // GRID_W
    assert d == D_MODEL and seq % (B_QROWS * GRID_W) == 0 and rows // B_QROWS >= 3

    c_pad = jnp.pad(c, ((0, SUBLANES - bsz), (0, 0)))
    mod = _ada_call(c_pad, w_ada, b_ada)[:, :bsz].reshape(DEPTH * bsz * 6, 1, d)
    cos_t, sin_lo, sin_hi = _rope_tables(seq)
    bias_tiles = _nbr_bias_tables(b_rpb, rows)

    w_front = w_in[:, :, :OFF_G].astype(BF16)
    w_gate = w_in[:, :, OFF_G:].astype(BF16)
    b_gate3 = b_gate.reshape(DEPTH, 1, N_BRANCH * d)
    wbr16 = w_branch.astype(BF16)
    wo16 = w_out.astype(BF16)
    ln_g2 = ln_g.reshape(DEPTH * 2, 1, d)
    ln_b2 = ln_b.reshape(DEPTH * 2, 1, d)
    wq16 = p_wq.astype(BF16)
    sk16 = p_subkeys.astype(BF16).reshape(DEPTH, 2 * PEER_HEADS, PEER_KEYS, PEER_QDIM // 2)
    u16 = p_u.astype(BF16)
    vt16 = jnp.swapaxes(p_v, 1, 2).astype(BF16)
    blocks = jax.vmap(jax.vmap(_block_diag))
    lru_w = jnp.concatenate([blocks(d_wa), blocks(d_wx)], axis=-1).astype(BF16)
    lru_b = jnp.concatenate([d_ba, d_bx], axis=-1)
    conv_b = d_conv_b.reshape(DEPTH, 1, D_WIDTH)
    norm_g = c_norm_g.reshape(DEPTH, 1, C_V)

    for l in range(DEPTH):
        aq, akv, bq, bk, bv, cq, ck, cv, dx, dg = _inproj_call(
            x, mod, w_front, l, cos_t, sin_lo, sin_hi, _tile(seq, 512))
        oa = _window_call(a_sink, aq, akv, l)
        ob = _nbr_call(bq, bk, bv, bias_tiles, l)
        oc = _diff_call(c_lambda, norm_g, cq, ck, cv, l, _tile(seq, DIFF_TQ), _tile(seq, DIFF_TK))
        hf, hb = _lru_call(dx, d_conv_w, conv_b, lru_w, lru_b, d_lam, l, _tile(seq, 256))
        x1, h2 = _merge_call(x, mod, oa, ob, oc, hf, hb, dg, w_gate, b_gate3, wbr16, wo16,
                             ln_g2, ln_b2, l, _tile(seq, 256))
        h2f = h2.reshape(tokens, d)
        s1, s2, tau = _peer_route_call(h2f, wq16, sk16, l, _tile(tokens, 512))
        xn = _peer_mix_call(h2f, u16, vt16, l, s1, s2, tau, x1.reshape(tokens, d), mod,
                            ln_g2, ln_b2, seq, _tile(seq, 1024))
        x = xn.reshape(bsz, seq, d)
    return x
```

```python
import functools
import math

import numpy as np
import jax
import jax.numpy as jnp
from jax import lax
from jax.experimental import pallas as pl
from jax.experimental.pallas import tpu as pltpu

F32 = jnp.float32
BF16 = jnp.bfloat16

LANES = 128
SUBLANES = 8
VMEM_LIMIT = 56 * 1024 * 1024

D_MODEL = 1024
DEPTH = 4
GRID_W = 64
HEAD_DIM = 64
ROPE_THETA = 10000.0
NEG_INF = -1e30
LN_EPS = 1e-5
A_HEADS = 8
A_KV_HEADS = 2
A_WINDOW = 128
A_BLOCK = 128
B_HEADS = 8
B_KH = 8
B_KW = 16
C_HEADS = 4
C_VDIM = 2 * HEAD_DIM
D_WIDTH = 512
D_BLOCKS = 8
D_CONV = 4
LRU_C = 8.0
N_BRANCH = 4
BRANCH_W = 512
PEER_HEADS = 8
PEER_KEYS = 128
PEER_N = PEER_KEYS * PEER_KEYS
PEER_QDIM = 256
PEER_TOPK = 16
ALPHA = (2.0 * DEPTH) ** 0.25
LOG2_E = math.log2(math.e)

A_Q = A_HEADS * HEAD_DIM
A_KV = A_KV_HEADS * HEAD_DIM
B_QKV = B_HEADS * HEAD_DIM
C_QK = C_HEADS * 2 * HEAD_DIM
C_V = C_HEADS * C_VDIM
OFF_A = 0
OFF_B = A_Q + 2 * A_KV
OFF_C = OFF_B + 3 * B_QKV
OFF_D = OFF_C + 2 * C_QK + C_V
OFF_G = OFF_D + 2 * D_WIDTH

DIFF_TQ, DIFF_TK = 1024, 2048
A_QBLOCKS = 2

B_QROWS = 4
B_KROWS = B_QROWS + B_KH


def _params(sem):
    return pltpu.CompilerParams(dimension_semantics=sem, vmem_limit_bytes=VMEM_LIMIT)


def _gelu(x):
    c = math.sqrt(2.0 / math.pi)
    half = 0.5 * x
    return half + half * jnp.tanh(x * (c + (c * 0.044715) * (x * x)))


def _layer_norm(y, g, b):
    mu = jnp.mean(y, axis=-1, keepdims=True)
    yc = y - mu
    var = jnp.mean(yc * yc, axis=-1, keepdims=True)
    return yc * lax.rsqrt(var + LN_EPS) * g + b


def _dot_nt(a, b):
    return lax.dot_general(a, b, (((1,), (1,)), ((), ())), preferred_element_type=F32)


def _low_half_mask():
    return lax.broadcasted_iota(jnp.int32, (1, LANES), 1) < HEAD_DIM


def _ada_kernel(c_ref, w_ref, b_ref, o_ref):
    c = c_ref[...]
    ca = c * jax.nn.sigmoid(c)
    o_ref[0] = jnp.dot(ca, w_ref[0], precision=lax.Precision.HIGHEST,
                       preferred_element_type=F32) + b_ref[0]


def _ada_call(c_pad, w_ada, b_ada):
    depth, d, cols = w_ada.shape
    rows = c_pad.shape[0]
    tn = 1024
    return pl.pallas_call(
        _ada_kernel,
        grid=(depth, cols // tn),
        in_specs=[pl.BlockSpec((rows, d), lambda l, j: (0, 0)),
                  pl.BlockSpec((1, d, tn), lambda l, j: (l, 0, j)),
                  pl.BlockSpec((1, 1, tn), lambda l, j: (l, 0, j))],
        out_specs=pl.BlockSpec((1, rows, tn), lambda l, j: (l, 0, j)),
        out_shape=jax.ShapeDtypeStruct((depth, rows, cols), F32),
        compiler_params=_params(("arbitrary", "arbitrary")),
        name="adaln_mod",
    )(c_pad, w_ada, b_ada.reshape(depth, 1, cols))


def _inproj_kernel(x_ref, sh_ref, sc_ref, w_ref, cos_ref, slo_ref, shi_ref,
                   aq_ref, akv_ref, bq_ref, bk_ref, bv_ref, cq_ref, ck_ref, cv_ref,
                   dx_ref, dg_ref):
    h = (x_ref[0] * (1.0 + sc_ref[...]) + sh_ref[...]).astype(BF16)
    cos = cos_ref[...]
    slo = slo_ref[...]
    shi = shi_ref[...]
    qscale = HEAD_DIM ** -0.5

    def rope(v):
        return (v * cos + pltpu.roll(v, LANES - HEAD_DIM // 2, 1) * slo
                + pltpu.roll(v, HEAD_DIM // 2, 1) * shi)

    def tiles(y, start, n):
        return [y[:, (start + i) * LANES:(start + i + 1) * LANES] for i in range(n)]

    ya = jnp.dot(h, w_ref[:, OFF_A:OFF_B], preferred_element_type=F32)
    for i, t in enumerate(tiles(ya, 0, A_Q // LANES)):
        aq_ref[0, :, i * LANES:(i + 1) * LANES] = (rope(t) * qscale).astype(BF16)
    ak = rope(ya[:, A_Q:A_Q + LANES])
    av = ya[:, A_Q + LANES:A_Q + 2 * LANES]
    for i, t in enumerate((ak, av, pltpu.roll(ak, HEAD_DIM, 1), pltpu.roll(av, HEAD_DIM, 1))):
        akv_ref[0, :, i * LANES:(i + 1) * LANES] = t.astype(BF16)

    yb = jnp.dot(h, w_ref[:, OFF_B:OFF_C], preferred_element_type=F32)
    bq_ref[0] = (yb[:, 0:B_QKV] * qscale).astype(BF16)
    bk_ref[0] = yb[:, B_QKV:2 * B_QKV].astype(BF16)
    bv_ref[0] = yb[:, 2 * B_QKV:3 * B_QKV].astype(BF16)

    yc = jnp.dot(h, w_ref[:, OFF_C:OFF_D], preferred_element_type=F32)
    nt = C_QK // LANES
    for i, t in enumerate(tiles(yc, 0, nt)):
        cq_ref[0, :, i * LANES:(i + 1) * LANES] = (rope(t) * qscale).astype(BF16)
    for i, t in enumerate(tiles(yc, nt, nt)):
        ck_ref[0, :, i * LANES:(i + 1) * LANES] = rope(t).astype(BF16)
    cv_ref[0] = yc[:, 2 * C_QK:2 * C_QK + C_V].astype(BF16)

    yd = jnp.dot(h, w_ref[:, OFF_D:OFF_G], preferred_element_type=F32)
    dx_ref[0] = yd[:, 0:D_WIDTH]
    dg_ref[0] = yd[:, D_WIDTH:2 * D_WIDTH]


def _mod_spec(layer, which, bsz, d):
    return pl.BlockSpec((None, 1, d), lambda b, *_: ((layer * bsz + b) * 6 + which, 0, 0))


def _inproj_call(x, mod, w_front, layer, cos_t, sin_lo, sin_hi, tm):
    bsz, seq, d = x.shape
    tok = lambda w: pl.BlockSpec((1, tm, w), lambda b, i: (b, i, 0))
    tab = pl.BlockSpec((tm, LANES), lambda b, i: (i, 0))
    outs = [(A_Q, BF16), (4 * LANES, BF16), (B_QKV, BF16), (B_QKV, BF16), (B_QKV, BF16),
            (C_QK, BF16), (C_QK, BF16), (C_V, BF16), (D_WIDTH, F32), (D_WIDTH, F32)]
    return pl.pallas_call(
        _inproj_kernel,
        grid=(bsz, seq // tm),
        in_specs=[tok(d), _mod_spec(layer, 0, bsz, d), _mod_spec(layer, 1, bsz, d),
                  pl.BlockSpec((None,) + w_front.shape[1:], lambda b, i: (layer, 0, 0)), tab, tab, tab],
        out_specs=[tok(w) for w, _ in outs],
        out_shape=[jax.ShapeDtypeStruct((bsz, seq, w), dt) for w, dt in outs],
        compiler_params=_params(("parallel", "arbitrary")),
        name="in_proj",
    )(x, mod, mod, w_front, cos_t, sin_lo, sin_hi)


def _window_kernel(sink_ref, q_ref, *rest, seq, layer):
    kv_refs, o_ref = rest[:-1], rest[-1]
    n = pl.program_id(1)
    blk = A_BLOCK * A_QBLOCKS
    nk = A_BLOCK * len(kv_refs)
    kv = jnp.concatenate([r[0] for r in kv_refs], axis=0)
    k_tiles = (kv[:, 0:LANES], kv[:, 2 * LANES:3 * LANES])
    v_tiles = (kv[:, LANES:2 * LANES], kv[:, 3 * LANES:4 * LANES])
    qi = lax.broadcasted_iota(jnp.int32, (2 * blk, nk), 0) % blk
    kj = lax.broadcasted_iota(jnp.int32, (2 * blk, nk), 1)
    rel = kj - A_BLOCK - qi
    kpos = n * blk + kj - A_BLOCK
    valid = (jnp.abs(rel) <= A_WINDOW) & (kpos >= 0) & (kpos < seq)
    low_rows = lax.broadcasted_iota(jnp.int32, (2 * blk, 1), 0) < blk
    low = _low_half_mask()
    heads_per_kv = A_HEADS // A_KV_HEADS
    for pair in range(A_HEADS // 2):
        kv_head = (2 * pair) // heads_per_kv
        lo_src, hi_src = kv_head, 1 - kv_head
        qp = q_ref[0, :, pair * LANES:(pair + 1) * LANES]
        zero = jnp.zeros_like(qp)
        s = jnp.concatenate([_dot_nt(jnp.where(low, qp, zero), k_tiles[lo_src]),
                             _dot_nt(jnp.where(low, zero, qp), k_tiles[hi_src])], axis=0)
        s = jnp.where(valid, s, NEG_INF)
        sk = jnp.where(low_rows, sink_ref[layer, 2 * pair], sink_ref[layer, 2 * pair + 1])
        m = jnp.maximum(jnp.max(s, axis=-1, keepdims=True), sk)
        p = jnp.exp(s - m)
        p = (p / (jnp.sum(p, axis=-1, keepdims=True) + jnp.exp(sk - m))).astype(BF16)
        o_lo = jnp.dot(p[0:blk], v_tiles[lo_src], preferred_element_type=F32)
        o_hi = jnp.dot(p[blk:], v_tiles[hi_src], preferred_element_type=F32)
        o_ref[0, :, pair * LANES:(pair + 1) * LANES] = jnp.where(low, o_lo, o_hi).astype(BF16)


def _window_call(sink, aq, akv, layer):
    bsz, seq, _ = aq.shape
    nb = seq // A_BLOCK
    nq = A_BLOCK * A_QBLOCKS
    assert seq % nq == 0
    kvspec = lambda i: pl.BlockSpec((1, A_BLOCK, 4 * LANES),
                                    lambda b, n: (b, jnp.clip(n * A_QBLOCKS - 1 + i, 0, nb - 1), 0))
    nkv = A_QBLOCKS + 2
    return pl.pallas_call(
        functools.partial(_window_kernel, seq=seq, layer=layer),
        grid=(bsz, seq // nq),
        in_specs=[pl.BlockSpec(memory_space=pltpu.SMEM),
                  pl.BlockSpec((1, nq, A_Q), lambda b, n: (b, n, 0))] + [kvspec(i) for i in range(nkv)],
        out_specs=pl.BlockSpec((1, nq, A_Q), lambda b, n: (b, n, 0)),
        out_shape=jax.ShapeDtypeStruct((bsz, seq, A_Q), BF16),
        compiler_params=_params(("parallel", "arbitrary")),
        name="window_gqa",
    )(sink, aq, *([akv] * nkv))


def _nbr_bias_tables(rpb, rows):
    groups = rows // B_QROWS
    qr = np.arange(B_QROWS)
    kr = np.arange(B_KROWS)
    row_onehot = np.zeros((3, B_QROWS, B_KROWS, 2 * B_KH - 1), np.float32)
    row_valid = np.zeros((3, B_QROWS, B_KROWS), bool)
    for var, g in enumerate((0, 1, groups - 1)):
        base = int(np.clip(g - 1, 0, groups - 3)) * B_QROWS
        r = g * B_QROWS + qr
        rstart = np.clip(r - B_KH // 2, 0, rows - B_KH)
        krow = base + kr
        ok = (krow[None, :] >= rstart[:, None]) & (krow[None, :] < rstart[:, None] + B_KH)
        drow = np.clip(krow[None, :] - r[:, None] + (B_KH - 1), 0, 2 * B_KH - 2)
        row_valid[var] = ok
        row_onehot[var, qr[:, None], kr[None, :], drow] = 1.0
    qc = np.arange(GRID_W)
    kc = np.arange(GRID_W)
    cstart = np.clip(qc - B_KW // 2, 0, GRID_W - B_KW)
    col_valid = (kc[None, :] >= cstart[:, None]) & (kc[None, :] < cstart[:, None] + B_KW)
    dcol = np.clip(kc[None, :] - qc[:, None], 1 - B_KW, B_KW - 1) + (B_KW - 1)
    col_onehot = np.zeros((GRID_W, GRID_W, 2 * B_KW - 1), np.float32)
    col_onehot[qc[:, None], kc[None, :], dcol] = 1.0
    valid = row_valid[:, :, None, :, None] & col_valid[None, None, :, None, :]
    hi = lax.Precision.HIGHEST
    by_col = jnp.einsum('lhab,qkb->lhaqk', rpb, jnp.asarray(col_onehot), precision=hi)
    tiles = jnp.einsum('lhaqk,vrsa->lvhrqsk', by_col, jnp.asarray(row_onehot), precision=hi)
    tiles = jnp.where(jnp.asarray(valid)[None, :, None], tiles, NEG_INF)
    depth = rpb.shape[0]
    return tiles.reshape(depth, 3, B_HEADS, B_QROWS * GRID_W, B_KROWS * GRID_W)


def _nbr_kernel(q_ref, k0_ref, k1_ref, k2_ref, v0_ref, v1_ref, v2_ref, bias_ref, o_ref):
    nq = B_QROWS * GRID_W
    k = jnp.concatenate([k0_ref[0], k1_ref[0], k2_ref[0]], axis=0)
    v = jnp.concatenate([v0_ref[0], v1_ref[0], v2_ref[0]], axis=0)
    low = _low_half_mask()
    for pair in range(B_HEADS // 2):
        cols = slice(pair * LANES, (pair + 1) * LANES)
        qp = q_ref[0, :, cols]
        zero = jnp.zeros_like(qp)
        q2 = jnp.concatenate([jnp.where(low, qp, zero), jnp.where(low, zero, qp)], axis=0)
        bias = jnp.concatenate([bias_ref[0, 0, 2 * pair], bias_ref[0, 0, 2 * pair + 1]], axis=0)
        s = _dot_nt(q2, k[:, cols]) + bias
        tiles = [s[:, c * LANES:(c + 1) * LANES] for c in range(s.shape[1] // LANES)]
        part = tiles[0]
        for t in tiles[1:]:
            part = jnp.maximum(part, t)
        m = jnp.broadcast_to(jnp.max(part, axis=-1, keepdims=True), part.shape)
        p = jnp.concatenate([jnp.exp(t - m) for t in tiles], axis=1).astype(BF16)
        v_ext = jnp.concatenate([v[:, cols], jnp.ones((v.shape[0], LANES), BF16)], axis=1)
        o_ext = jnp.dot(p, v_ext, preferred_element_type=F32)
        o2 = o_ext[:, 0:LANES] / o_ext[:, LANES:]
        o_ref[0, :, cols] = jnp.where(low, o2[0:nq], o2[nq:]).astype(BF16)


def _nbr_call(bq, bk, bv, bias_tiles, layer):
    bsz, seq, w = bq.shape
    nq = B_QROWS * GRID_W
    groups = seq // nq
    start = lambda g: jnp.clip(g - 1, 0, groups - 3)
    kvspec = lambda off: pl.BlockSpec((1, nq, w), lambda b, g: (b, start(g) + off, 0))
    variant = lambda g: jnp.where(g == 0, 0, jnp.where(g == groups - 1, 2, 1))
    return pl.pallas_call(
        _nbr_kernel,
        grid=(bsz, groups),
        in_specs=[pl.BlockSpec((1, nq, w), lambda b, g: (b, g, 0)),
                  kvspec(0), kvspec(1), kvspec(2), kvspec(0), kvspec(1), kvspec(2),
                  pl.BlockSpec((1, 1, B_HEADS, nq, B_KROWS * GRID_W),
                               lambda b, g: (layer, variant(g), 0, 0, 0))],
        out_specs=pl.BlockSpec((1, nq, w), lambda b, g: (b, g, 0)),
        out_shape=jax.ShapeDtypeStruct((bsz, seq, w), BF16),
        compiler_params=_params(("parallel", "arbitrary")),
        name="neighbourhood_attn",
    )(bq, bk, bk, bk, bv, bv, bv, bias_tiles)


def _diff_kernel(lam_ref, ng_ref, q_ref, k_ref, v_ref, o_ref, m_ref, acc_ref, *, lam_init):
    j = pl.program_id(3)
    tq = q_ref.shape[1]
    tk = k_ref.shape[1]

    @pl.when(j == 0)
    def _():
        m_ref[...] = jnp.full(m_ref.shape, -jnp.inf, F32)
        acc_ref[...] = jnp.zeros(acc_ref.shape, F32)

    low = _low_half_mask()
    q = q_ref[0]
    k = k_ref[0]
    zero = jnp.zeros_like(q)
    v_ext = jnp.concatenate([v_ref[0], jnp.ones((tk, LANES), BF16)], axis=1)
    scores = [_dot_nt(qm, k) for qm in (jnp.where(low, q, zero), jnp.where(low, zero, q))]
    for mp, s in enumerate(scores):
        tiles = [s[:, c * LANES:(c + 1) * LANES] for c in range(tk // LANES)]
        part = tiles[0]
        for t in tiles[1:]:
            part = jnp.maximum(part, t)
        m_old = m_ref[mp]
        m_new = jnp.maximum(m_old, jnp.broadcast_to(jnp.max(part, axis=-1, keepdims=True), (tq, LANES)))
        alpha = jnp.exp(m_old - m_new)
        p = jnp.concatenate([jnp.exp(t - m_new) for t in tiles], axis=1).astype(BF16)
        acc_ref[mp] = (jnp.concatenate([alpha, alpha], axis=1) * acc_ref[mp]
                       + jnp.dot(p, v_ext, preferred_element_type=F32))
        m_ref[mp] = m_new

    @pl.when(j == pl.num_programs(3) - 1)
    def _():
        lf = lam_ref[...]
        lam = (jnp.exp(jnp.sum(lf[0:1] * lf[1:2], axis=-1, keepdims=True))
               - jnp.exp(jnp.sum(lf[2:3] * lf[3:4], axis=-1, keepdims=True)) + lam_init)
        a1 = acc_ref[0]
        a2 = acc_ref[1]
        o = a1[:, 0:C_VDIM] / a1[:, C_VDIM:] - lam * (a2[:, 0:C_VDIM] / a2[:, C_VDIM:])
        o = o * lax.rsqrt(jnp.mean(o * o, axis=-1, keepdims=True) + LN_EPS)
        o_ref[0] = (o * ng_ref[...] * (1.0 - lam_init)).astype(BF16)


def _diff_call(lam, norm_g, cq, ck, cv, layer, tq, tk):
    bsz, seq, _ = cq.shape
    lam_init = 0.8 - 0.6 * math.exp(-0.3 * layer)
    return pl.pallas_call(
        functools.partial(_diff_kernel, lam_init=lam_init),
        grid=(bsz, C_HEADS, seq // tq, seq // tk),
        in_specs=[pl.BlockSpec((None, 4, HEAD_DIM), lambda b, h, i, j: (layer, 0, 0)),
                  pl.BlockSpec((None, 1, C_VDIM), lambda b, h, i, j: (layer, 0, h)),
                  pl.BlockSpec((1, tq, LANES), lambda b, h, i, j: (b, i, h)),
                  pl.BlockSpec((1, tk, LANES), lambda b, h, i, j: (b, j, h)),
                  pl.BlockSpec((1, tk, C_VDIM), lambda b, h, i, j: (b, j, h))],
        out_specs=pl.BlockSpec((1, tq, C_VDIM), lambda b, h, i, j: (b, i, h)),
        out_shape=jax.ShapeDtypeStruct((bsz, seq, C_V), BF16),
        scratch_shapes=[pltpu.VMEM((2, tq, LANES), F32), pltpu.VMEM((2, tq, C_VDIM + LANES), F32)],
        compiler_params=_params(("parallel", "parallel", "parallel", "arbitrary")),
        name="diff_attn",
    )(lam, norm_g, cq, ck, cv)


def _lru_kernel(xf_ref, xfp_ref, xfn_ref, xb_ref, xbp_ref, xbn_ref, cw_ref, cb_ref,
                w_ref, bias_ref, lam_ref, hf_ref, hb_ref, ext_ref, carry_ref, sa_ref, sb_ref, *, tc):
    j = pl.program_id(1)
    nc = pl.num_programs(1)
    halo = 8
    left = D_CONV // 2

    @pl.when(j == 0)
    def _():
        carry_ref[...] = jnp.zeros(carry_ref.shape, F32)

    z = -lam_ref[...]
    softplus = jnp.maximum(z, 0.0) + jnp.log1p(jnp.exp(-jnp.abs(z)))

    def gates(d, cur_ref, prev_ref, next_ref, chunk):
        ext_ref[0:halo] = jnp.where(chunk > 0, prev_ref[0], 0.0)
        ext_ref[halo:halo + tc] = cur_ref[0]
        ext_ref[halo + tc:2 * halo + tc] = jnp.where(chunk < nc - 1, next_ref[0], 0.0)
        xc = cb_ref[...]
        for t in range(D_CONV):
            xc = xc + cw_ref[t:t + 1] * ext_ref[pl.ds(halo - left + t, tc), :]
        pre = jnp.dot(xc.astype(BF16), w_ref[d], preferred_element_type=F32) + bias_ref[d:d + 1]
        r = jax.nn.sigmoid(pre[:, 0:D_WIDTH])
        i = jax.nn.sigmoid(pre[:, D_WIDTH:2 * D_WIDTH])
        log_a = (-LRU_C) * r * softplus[d:d + 1]
        a = jnp.exp(log_a)
        y = -jnp.tanh(log_a) * (a * a + 1.0)
        root = jnp.where(y > 0.0, y * lax.rsqrt(y), 0.0)
        return a, root * (i * xc)

    sub = lax.broadcasted_iota(jnp.int32, (1, SUBLANES, 1), 1)
    groups = tc // SUBLANES

    def scan(a, b, reverse, carry_row, h_ref):
        width = a.shape[1]
        a = a.reshape(groups, SUBLANES, width)
        b = b.reshape(groups, SUBLANES, width)
        step = 1
        while step < SUBLANES:
            if reverse:
                keep = sub < SUBLANES - step
                shift = SUBLANES - step
            else:
                keep = sub >= step
                shift = step
            a_sh = jnp.where(keep, pltpu.roll(a, shift, 1), 1.0)
            b_sh = jnp.where(keep, pltpu.roll(b, shift, 1), 0.0)
            b = a * b_sh + b
            a = a * a_sh
            step *= 2
        sa_ref[...] = a.reshape(tc, width)
        sb_ref[...] = b.reshape(tc, width)
        h_in = carry_ref[carry_row:carry_row + 1]
        for g in (range(groups - 1, -1, -1) if reverse else range(groups)):
            rows_g = slice(g * SUBLANES, (g + 1) * SUBLANES)
            h_g = sb_ref[rows_g] + sa_ref[rows_g] * h_in
            h_ref[0, rows_g] = h_g
            h_in = h_g[0:1] if reverse else h_g[SUBLANES - 1:SUBLANES]
        carry_ref[carry_row:carry_row + 1] = h_in

    a, b = gates(0, xf_ref, xfp_ref, xfn_ref, j)
    scan(a, b, False, 0, hf_ref)
    a, b = gates(1, xb_ref, xbp_ref, xbn_ref, nc - 1 - j)
    scan(a, b, True, 1, hb_ref)


def _lru_call(dx, conv_w, conv_b, w_gates, b_gates, lam, layer, tc):
    bsz, seq, w = dx.shape
    nc = seq // tc
    hb = tc // 8
    nh = seq // 8
    cur = lambda f: pl.BlockSpec((1, tc, w), f)
    halo = lambda f: pl.BlockSpec((1, 8, w), f)
    full = lambda a: pl.BlockSpec((None,) + a.shape[1:], lambda b, j: (layer,) + (0,) * (a.ndim - 1))
    mirror = lambda j: nc - 1 - j
    return pl.pallas_call(
        functools.partial(_lru_kernel, tc=tc),
        grid=(bsz, nc),
        in_specs=[cur(lambda b, j: (b, j, 0)),
                  halo(lambda b, j: (b, jnp.maximum(j * hb - 1, 0), 0)),
                  halo(lambda b, j: (b, jnp.minimum((j + 1) * hb, nh - 1), 0)),
                  cur(lambda b, j: (b, mirror(j), 0)),
                  halo(lambda b, j: (b, jnp.maximum(mirror(j) * hb - 1, 0), 0)),
                  halo(lambda b, j: (b, jnp.minimum((mirror(j) + 1) * hb, nh - 1), 0)),
                  full(conv_w), full(conv_b), full(w_gates), full(b_gates), full(lam)],
        out_specs=[cur(lambda b, j: (b, j, 0)), cur(lambda b, j: (b, mirror(j), 0))],
        out_shape=[jax.ShapeDtypeStruct((bsz, seq, w), F32)] * 2,
        scratch_shapes=[pltpu.VMEM((tc + 16, w), F32), pltpu.VMEM((2, w), F32),
                        pltpu.VMEM((tc, w), F32), pltpu.VMEM((tc, w), F32)],
        compiler_params=_params(("parallel", "arbitrary")),
        name="rg_lru",
    )(dx, dx, dx, dx, dx, dx, conv_w, conv_b, w_gates, b_gates, lam)


def _merge_kernel(x_ref, sh1_ref, sc1_ref, g1_ref, sh2_ref, sc2_ref,
                  oa_ref, ob_ref, oc_ref, hf_ref, hb_ref, dg_ref,
                  wg_ref, bg_ref, wbr_ref, wo_ref, lng_ref, lnb_ref,
                  x1_ref, h2_ref):
    x = x_ref[0]
    h = (x * (1.0 + sc1_ref[...]) + sh1_ref[...]).astype(BF16)
    od = ((hf_ref[0] + hb_ref[0]) * _gelu(dg_ref[0])).astype(BF16)
    branches = (oa_ref[0], ob_ref[0], oc_ref[0], od)
    mixed = None
    for n in range(N_BRANCH):
        cols = slice(n * D_MODEL, (n + 1) * D_MODEL)
        gate = jax.nn.sigmoid(jnp.dot(h, wg_ref[:, cols], preferred_element_type=F32) + bg_ref[:, cols])
        term = gate * jnp.dot(branches[n], wbr_ref[n], preferred_element_type=F32)
        mixed = term if mixed is None else mixed + term
    mix = jnp.dot(mixed.astype(BF16), wo_ref[...], preferred_element_type=F32)
    x1 = _layer_norm(ALPHA * x + g1_ref[...] * mix, lng_ref[...], lnb_ref[...])
    x1_ref[0] = x1
    h2_ref[0] = (x1 * (1.0 + sc2_ref[...]) + sh2_ref[...]).astype(BF16)


def _merge_call(x, mod, oa, ob, oc, hf, hb, dg, wg, bg, wbr, wo, lng, lnb, layer, tm):
    bsz, seq, d = x.shape
    tok = lambda w: pl.BlockSpec((1, tm, w), lambda b, i: (b, i, 0))
    full = lambda a: pl.BlockSpec((None,) + a.shape[1:], lambda b, i: (layer,) + (0,) * (a.ndim - 1))
    ln = pl.BlockSpec((None, 1, d), lambda b, i: (2 * layer, 0, 0))
    return pl.pallas_call(
        _merge_kernel,
        grid=(bsz, seq // tm),
        in_specs=[tok(d)] + [_mod_spec(layer, which, bsz, d) for which in range(5)] + [tok(BRANCH_W)] * 6
                 + [full(wg), full(bg), full(wbr), full(wo), ln, ln],
        out_specs=[tok(d), tok(d)],
        out_shape=[jax.ShapeDtypeStruct((bsz, seq, d), F32),
                   jax.ShapeDtypeStruct((bsz, seq, d), BF16)],
        compiler_params=_params(("parallel", "arbitrary")),
        name="merge_out_ln",
    )(x, mod, mod, mod, mod, mod, oa, ob, oc, hf, hb, dg, wg, bg, wbr, wo, lng, lnb)


def _top_values(work, count):
    vals = []
    for _ in range(count):
        mx = jnp.max(work, axis=0, keepdims=True)
        vals.append(mx)
        work = jnp.where(work == mx, -jnp.inf, work)
    return vals


def _peer_route_kernel(h2_ref, wq_ref, sk_ref, s1_ref, s2_ref, tau_ref):
    q = jnp.dot(h2_ref[...], wq_ref[...], preferred_element_type=F32).astype(BF16)
    k = PEER_TOPK
    pad = jnp.full((SUBLANES - 1, q.shape[0]), -jnp.inf, F32)
    for head in range(PEER_HEADS):
        scores = []
        top = []
        for part in range(2):
            hp = 2 * head + part
            st = _dot_nt(sk_ref[hp], q[:, hp * LANES:(hp + 1) * LANES]) * LOG2_E
            scores.append(st)
            top.append(jnp.concatenate(_top_values(st, k + 1) + [pad], axis=0))
        sv1, sv2 = top
        cand = [sv1[0:1] + sv2]
        cand += [sv1[a:a + 1] + sv2[0:SUBLANES] for a in range(1, SUBLANES)]
        cand += [sv1[SUBLANES:] + sv2[0:1]]
        best = _top_values(jnp.concatenate(cand, axis=0), k + 1)
        m = best[0]
        zsum = jnp.zeros_like(m)
        for t in best[:k]:
            zsum = zsum + jnp.exp2(t - m)
        c = m + jnp.log2(zsum)
        s1_ref[head] = scores[0] - c
        s2_ref[head] = scores[1]
        tau_ref[head:head + 1] = 0.5 * (best[k - 1] + best[k]) - c


def _peer_route_call(h2, wq, subkeys, layer, tm):
    tokens, d = h2.shape
    score_spec = pl.BlockSpec((PEER_HEADS, PEER_KEYS, tm), lambda i: (0, 0, i))
    stat_spec = pl.BlockSpec((PEER_HEADS, tm), lambda i: (0, i))
    score_shape = jax.ShapeDtypeStruct((PEER_HEADS, PEER_KEYS, tokens), F32)
    stat_shape = jax.ShapeDtypeStruct((PEER_HEADS, tokens), F32)
    return pl.pallas_call(
        _peer_route_kernel,
        grid=(tokens // tm,),
        in_specs=[pl.BlockSpec((tm, d), lambda i: (i, 0)),
                  pl.BlockSpec((None,) + wq.shape[1:], lambda i: (layer, 0, 0)),
                  pl.BlockSpec((None,) + subkeys.shape[1:], lambda i: (layer, 0, 0, 0))],
        out_specs=[score_spec, score_spec, stat_spec],
        out_shape=[score_shape, score_shape, stat_shape],
        compiler_params=_params(("parallel",)),
        name="peer_route",
    )(h2, wq, subkeys)


def _peer_mix_kernel(h2_ref, ux_ref, uy_ref, vtx_ref, vty_ref, s1x_ref, s1y_ref, s2_ref, tau_ref,
                     x1_ref, g2_ref, lng_ref, lnb_ref, o_ref,
                     acc_ref, stx_ref, sty_ref, wx_ref, wy_ref, *, tt):
    j = pl.program_id(1)

    @pl.when(j == 0)
    def _():
        acc_ref[...] = jnp.zeros(acc_ref.shape, F32)
        sty_ref[...] = jnp.zeros(sty_ref.shape, F32)
        wx_ref[...] = jnp.zeros(wx_ref.shape, BF16)

    per_tile = SUBLANES // 2
    width = 2 * LANES

    def half_step(u_ref, st_out, st_in, s1_ref, row0, w_out, vt_ref, w_in):
        for c in range(tt // width):
            cols = slice(c * width, (c + 1) * width)
            st_out[:, cols] = _dot_nt(u_ref[...], h2_ref[cols, :])
            for half in range(width // LANES):
                lanes = slice(c * width + half * LANES, c * width + (half + 1) * LANES)
                for a in range(per_tile):
                    g = jnp.zeros((PEER_KEYS, LANES), F32)
                    for head in range(PEER_HEADS):
                        zsum = s1_ref[head, row0 + a:row0 + a + 1, lanes] + s2_ref[head, :, lanes]
                        g = g + jnp.where(zsum >= tau_ref[head:head + 1, lanes], jnp.exp2(zsum), 0.0)
                    rows = slice(a * PEER_KEYS, (a + 1) * PEER_KEYS)
                    w_out[rows, lanes] = _gelu(st_in[rows, lanes].astype(BF16)) * g.astype(BF16)
            acc_ref[:, cols] += jnp.dot(vt_ref[...], w_in[:, cols], preferred_element_type=F32)

    half_step(ux_ref, stx_ref, sty_ref, s1x_ref, per_tile, wy_ref, vtx_ref, wx_ref)
    half_step(uy_ref, sty_ref, stx_ref, s1y_ref, 0, wx_ref, vty_ref, wy_ref)

    @pl.when(j == pl.num_programs(1) - 1)
    def _():
        y = acc_ref[...].T
        o_ref[...] = _layer_norm(ALPHA * x1_ref[...] + g2_ref[...] * y, lng_ref[...], lnb_ref[...])


def _peer_mix_call(h2, u, vt, layer, s1, s2, tau, x1, mod, lng, lnb, seq, tt):
    tokens, d = h2.shape
    bsz = tokens // seq
    tn = (SUBLANES // 2) * PEER_KEYS
    n_tiles = u.shape[1] // tn
    pairs = n_tiles // 2
    per_batch = seq // tt
    assert u.shape[1] % (2 * tn) == 0 and tt % (2 * LANES) == 0
    clamp = lambda t: jnp.clip(t, 0, n_tiles - 1)
    u_spec = lambda f: pl.BlockSpec((None, tn, d), lambda i, j: (layer, clamp(f(j)), 0))
    vt_spec = lambda f: pl.BlockSpec((None, d, tn), lambda i, j: (layer, 0, clamp(f(j))))
    row_spec = lambda f: pl.BlockSpec((PEER_HEADS, SUBLANES, tt), lambda i, j: (0, jnp.clip(f(j), 0, pairs - 1), i))
    key_spec = pl.BlockSpec((PEER_HEADS, PEER_KEYS, tt), lambda i, j: (0, 0, i))
    stat_spec = pl.BlockSpec((PEER_HEADS, tt), lambda i, j: (0, i))
    return pl.pallas_call(
        functools.partial(_peer_mix_kernel, tt=tt),
        grid=(tokens // tt, pairs + 1),
        in_specs=[pl.BlockSpec((tt, d), lambda i, j: (i, 0)),
                  u_spec(lambda j: 2 * j), u_spec(lambda j: 2 * j + 1),
                  vt_spec(lambda j: 2 * j - 2), vt_spec(lambda j: 2 * j - 1),
                  row_spec(lambda j: j - 1), row_spec(lambda j: j),
                  key_spec, stat_spec,
                  pl.BlockSpec((tt, d), lambda i, j: (i, 0)),
                  pl.BlockSpec((None, 1, d), lambda i, j: ((layer * bsz + i // per_batch) * 6 + 5, 0, 0)),
                  pl.BlockSpec((None, 1, d), lambda i, j: (2 * layer + 1, 0, 0)),
                  pl.BlockSpec((None, 1, d), lambda i, j: (2 * layer + 1, 0, 0))],
        out_specs=pl.BlockSpec((tt, d), lambda i, j: (i, 0)),
        out_shape=jax.ShapeDtypeStruct((tokens, d), F32),
        scratch_shapes=[pltpu.VMEM((d, tt), F32),
                        pltpu.VMEM((tn, tt), F32), pltpu.VMEM((tn, tt), F32),
                        pltpu.VMEM((tn, tt), BF16), pltpu.VMEM((tn, tt), BF16)],
        compiler_params=_params(("parallel", "arbitrary")),
        name="peer_mix",
    )(h2, u, u, vt, vt, s1, s1, s2, tau, x1, mod, lng, lnb)


def _rope_tables(seq):
    inv = 1.0 / (ROPE_THETA ** (jnp.arange(0, HEAD_DIM, 2, dtype=F32) / HEAD_DIM))
    ang = jnp.arange(seq, dtype=F32)[:, None] * inv[None, :]
    cos, sin = jnp.cos(ang), jnp.sin(ang)
    zero = jnp.zeros_like(sin)
    reps = LANES // HEAD_DIM
    cos_t = jnp.tile(jnp.concatenate([cos, cos], axis=1), (1, reps))
    sin_lo = jnp.tile(jnp.concatenate([-sin, zero], axis=1), (1, reps))
    sin_hi = jnp.tile(jnp.concatenate([zero, sin], axis=1), (1, reps))
    return cos_t, sin_lo, sin_hi


def _block_diag(w):
    nb, bs, _ = w.shape
    eye = jnp.eye(nb, dtype=w.dtype)
    return (eye[:, None, :, None] * w[:, :, None, :]).reshape(nb * bs, nb * bs)


def _tile(n, want):
    t = min(n, want)
    assert n % t == 0
    return t


def kernel(x, c, w_ada, b_ada, w_in, b_gate, a_sink, b_rpb, c_lambda, c_norm_g, d_conv_w, d_conv_b, d_wa, d_ba, d_wx, d_bx, d_lam, w_branch, w_out, ln_g, ln_b, p_wq, p_subkeys, p_u, p_v):
    bsz, seq, d = x.shape
    tokens = bsz * seq
    rows = seq // GRID_W
    assert d == D_MODEL and seq % (B_QROWS * GRID_W) == 0 and rows // B_QROWS >= 3

    c_pad = jnp.pad(c, ((0, SUBLANES - bsz), (0, 0)))
    mod = _ada_call(c_pad, w_ada, b_ada)[:, :bsz].reshape(DEPTH * bsz * 6, 1, d)
    cos_t, sin_lo, sin_hi = _rope_tables(seq)
    bias_tiles = _nbr_bias_tables(b_rpb, rows)

    w_front = w_in[:, :, :OFF_G].astype(BF16)
    w_gate = w_in[:, :, OFF_G:].astype(BF16)
    b_gate3 = b_gate.reshape(DEPTH, 1, N_BRANCH * d)
    wbr16 = w_branch.astype(BF16)
    wo16 = w_out.astype(BF16)
    ln_g2 = ln_g.reshape(DEPTH * 2, 1, d)
    ln_b2 = ln_b.reshape(DEPTH * 2, 1, d)
    wq16 = p_wq.astype(BF16)
    sk16 = p_subkeys.astype(BF16).reshape(DEPTH, 2 * PEER_HEADS, PEER_KEYS, PEER_QDIM // 2)
    u16 = p_u.astype(BF16)
    vt16 = jnp.swapaxes(p_v, 1, 2).astype(BF16)
    blocks = jax.vmap(jax.vmap(_block_diag))
    lru_w = jnp.concatenate([blocks(d_wa), blocks(d_wx)], axis=-1).astype(BF16)
    lru_b = jnp.concatenate([d_ba, d_bx], axis=-1)
    conv_b = d_conv_b.reshape(DEPTH, 1, D_WIDTH)
    norm_g = c_norm_g.reshape(DEPTH, 1, C_V)

    for l in range(DEPTH):
        aq, akv, bq, bk, bv, cq, ck, cv, dx, dg = _inproj_call(
            x, mod, w_front, l, cos_t, sin_lo, sin_hi, _tile(seq, 512))
        oa = _window_call(a_sink, aq, akv, l)
        ob = _nbr_call(bq, bk, bv, bias_tiles, l)
        oc = _diff_call(c_lambda, norm_g, cq, ck, cv, l, _tile(seq, DIFF_TQ), _tile(seq, DIFF_TK))
        hf, hb = _lru_call(dx, d_conv_w, conv_b, lru_w, lru_b, d_lam, l, _tile(seq, 256))
        x1, h2 = _merge_call(x, mod, oa, ob, oc, hf, hb, dg, w_gate, b_gate3, wbr16, wo16,
                             ln_g2, ln_b2, l, _tile(seq, 256))
        h2f = h2.reshape(tokens, d)
        s1, s2, tau = _peer_route_call(h2f, wq16, sk16, l, _tile(tokens, 512))
        xn = _peer_mix_call(h2f, u16, vt16, l, s1, s2, tau, x1.reshape(tokens, d), mod,
                            ln_g2, ln_b2, seq, _tile(seq, 1024))
        x = xn.reshape(bsz, seq, d)
    return x
```

```python
import functools
import math

import numpy as np
import jax
import jax.numpy as jnp
from jax import lax
from jax.experimental import pallas as pl
from jax.experimental.pallas import tpu as pltpu

F32 = jnp.float32
BF16 = jnp.bfloat16

LANES = 128
SUBLANES = 8
VMEM_LIMIT = 56 * 1024 * 1024

D_MODEL = 1024
DEPTH = 4
GRID_W = 64
HEAD_DIM = 64
ROPE_THETA = 10000.0
NEG_INF = -1e30
LN_EPS = 1e-5
A_HEADS = 8
A_KV_HEADS = 2
A_WINDOW = 128
A_BLOCK = 128
B_HEADS = 8
B_KH = 8
B_KW = 16
C_HEADS = 4
C_VDIM = 2 * HEAD_DIM
D_WIDTH = 512
D_BLOCKS = 8
D_CONV = 4
LRU_C = 8.0
N_BRANCH = 4
BRANCH_W = 512
PEER_HEADS = 8
PEER_KEYS = 128
PEER_N = PEER_KEYS * PEER_KEYS
PEER_QDIM = 256
PEER_TOPK = 16
ALPHA = (2.0 * DEPTH) ** 0.25
LOG2_E = math.log2(math.e)

A_Q = A_HEADS * HEAD_DIM
A_KV = A_KV_HEADS * HEAD_DIM
B_QKV = B_HEADS * HEAD_DIM
C_QK = C_HEADS * 2 * HEAD_DIM
C_V = C_HEADS * C_VDIM
OFF_A = 0
OFF_B = A_Q + 2 * A_KV
OFF_C = OFF_B + 3 * B_QKV
OFF_D = OFF_C + 2 * C_QK + C_V
OFF_G = OFF_D + 2 * D_WIDTH

DIFF_TQ, DIFF_TK = 1024, 2048
A_QBLOCKS = 2

B_QROWS = 4
B_KROWS = B_QROWS + B_KH


def _params(sem):
    return pltpu.CompilerParams(dimension_semantics=sem, vmem_limit_bytes=VMEM_LIMIT)


def _gelu(x):
    c = math.sqrt(2.0 / math.pi)
    half = 0.5 * x
    return half + half * jnp.tanh(x * (c + (c * 0.044715) * (x * x)))


def _layer_norm(y, g, b):
    mu = jnp.mean(y, axis=-1, keepdims=True)
    yc = y - mu
    var = jnp.mean(yc * yc, axis=-1, keepdims=True)
    return yc * lax.rsqrt(var + LN_EPS) * g + b


def _dot_nt(a, b):
    return lax.dot_general(a, b, (((1,), (1,)), ((), ())), preferred_element_type=F32)


def _low_half_mask():
    return lax.broadcasted_iota(jnp.int32, (1, LANES), 1) < HEAD_DIM


def _ada_kernel(c_ref, w_ref, b_ref, o_ref):
    c = c_ref[...]
    ca = c * jax.nn.sigmoid(c)
    o_ref[0] = jnp.dot(ca, w_ref[0], precision=lax.Precision.HIGHEST,
                       preferred_element_type=F32) + b_ref[0]


def _ada_call(c_pad, w_ada, b_ada):
    depth, d, cols = w_ada.shape
    rows = c_pad.shape[0]
    tn = 1024
    return pl.pallas_call(
        _ada_kernel,
        grid=(depth, cols // tn),
        in_specs=[pl.BlockSpec((rows, d), lambda l, j: (0, 0)),
                  pl.BlockSpec((1, d, tn), lambda l, j: (l, 0, j)),
                  pl.BlockSpec((1, 1, tn), lambda l, j: (l, 0, j))],
        out_specs=pl.BlockSpec((1, rows, tn), lambda l, j: (l, 0, j)),
        out_shape=jax.ShapeDtypeStruct((depth, rows, cols), F32),
        compiler_params=_params(("arbitrary", "arbitrary")),
        name="adaln_mod",
    )(c_pad, w_ada, b_ada.reshape(depth, 1, cols))


def _inproj_kernel(x_ref, sh_ref, sc_ref, w_ref, cos_ref, slo_ref, shi_ref,
                   aq_ref, akv_ref, bq_ref, bk_ref, bv_ref, cq_ref, ck_ref, cv_ref,
                   dx_ref, dg_ref):
    h = (x_ref[0] * (1.0 + sc_ref[...]) + sh_ref[...]).astype(BF16)
    cos = cos_ref[...]
    slo = slo_ref[...]
    shi = shi_ref[...]
    qscale = HEAD_DIM ** -0.5

    def rope(v):
        return (v * cos + pltpu.roll(v, LANES - HEAD_DIM // 2, 1) * slo
                + pltpu.roll(v, HEAD_DIM // 2, 1) * shi)

    def tiles(y, start, n):
        return [y[:, (start + i) * LANES:(start + i + 1) * LANES] for i in range(n)]

    ya = jnp.dot(h, w_ref[:, OFF_A:OFF_B], preferred_element_type=F32)
    for i, t in enumerate(tiles(ya, 0, A_Q // LANES)):
        aq_ref[0, :, i * LANES:(i + 1) * LANES] = (rope(t) * qscale).astype(BF16)
    ak = rope(ya[:, A_Q:A_Q + LANES])
    av = ya[:, A_Q + LANES:A_Q + 2 * LANES]
    for i, t in enumerate((ak, av, pltpu.roll(ak, HEAD_DIM, 1), pltpu.roll(av, HEAD_DIM, 1))):
        akv_ref[0, :, i * LANES:(i + 1) * LANES] = t.astype(BF16)

    yb = jnp.dot(h, w_ref[:, OFF_B:OFF_C], preferred_element_type=F32)
    bq_ref[0] = (yb[:, 0:B_QKV] * qscale).astype(BF16)
    bk_ref[0] = yb[:, B_QKV:2 * B_QKV].astype(BF16)
    bv_ref[0] = yb[:, 2 * B_QKV:3 * B_QKV].astype(BF16)

    yc = jnp.dot(h, w_ref[:, OFF_C:OFF_D], preferred_element_type=F32)
    nt = C_QK // LANES
    for i, t in enumerate(tiles(yc, 0, nt)):
        cq_ref[0, :, i * LANES:(i + 1) * LANES] = (rope(t) * qscale).astype(BF16)
    for i, t in enumerate(tiles(yc, nt, nt)):
        ck_ref[0, :, i * LANES:(i + 1) * LANES] = rope(t).astype(BF16)
    cv_ref[0] = yc[:, 2 * C_QK:2 * C_QK + C_V].astype(BF16)

    yd = jnp.dot(h, w_ref[:, OFF_D:OFF_G], preferred_element_type=F32)
    dx_ref[0] = yd[:, 0:D_WIDTH]
    dg_ref[0] = yd[:, D_WIDTH:2 * D_WIDTH]


def _mod_spec(layer, which, bsz, d):
    return pl.BlockSpec((None, 1, d), lambda b, *_: ((layer * bsz + b) * 6 + which, 0, 0))


def _inproj_call(x, mod, w_front, layer, cos_t, sin_lo, sin_hi, tm):
    bsz, seq, d = x.shape
    tok = lambda w: pl.BlockSpec((1, tm, w), lambda b, i: (b, i, 0))
    tab = pl.BlockSpec((tm, LANES), lambda b, i: (i, 0))
    outs = [(A_Q, BF16), (4 * LANES, BF16), (B_QKV, BF16), (B_QKV, BF16), (B_QKV, BF16),
            (C_QK, BF16), (C_QK, BF16), (C_V, BF16), (D_WIDTH, F32), (D_WIDTH, F32)]
    return pl.pallas_call(
        _inproj_kernel,
        grid=(bsz, seq // tm),
        in_specs=[tok(d), _mod_spec(layer, 0, bsz, d), _mod_spec(layer, 1, bsz, d),
                  pl.BlockSpec((None,) + w_front.shape[1:], lambda b, i: (layer, 0, 0)), tab, tab, tab],
        out_specs=[tok(w) for w, _ in outs],
        out_shape=[jax.ShapeDtypeStruct((bsz, seq, w), dt) for w, dt in outs],
        compiler_params=_params(("parallel", "arbitrary")),
        name="in_proj",
    )(x, mod, mod, w_front, cos_t, sin_lo, sin_hi)


def _window_kernel(sink_ref, q_ref, *rest, seq, layer):
    kv_refs, o_ref = rest[:-1], rest[-1]
    n = pl.program_id(1)
    blk = A_BLOCK * A_QBLOCKS
    nk = A_BLOCK * len(kv_refs)
    kv = jnp.concatenate([r[0] for r in kv_refs], axis=0)
    k_tiles = (kv[:, 0:LANES], kv[:, 2 * LANES:3 * LANES])
    v_tiles = (kv[:, LANES:2 * LANES], kv[:, 3 * LANES:4 * LANES])
    qi = lax.broadcasted_iota(jnp.int32, (2 * blk, nk), 0) % blk
    kj = lax.broadcasted_iota(jnp.int32, (2 * blk, nk), 1)
    rel = kj - A_BLOCK - qi
    kpos = n * blk + kj - A_BLOCK
    valid = (jnp.abs(rel) <= A_WINDOW) & (kpos >= 0) & (kpos < seq)
    low_rows = lax.broadcasted_iota(jnp.int32, (2 * blk, 1), 0) < blk
    low = _low_half_mask()
    heads_per_kv = A_HEADS // A_KV_HEADS
    for pair in range(A_HEADS // 2):
        kv_head = (2 * pair) // heads_per_kv
        lo_src, hi_src = kv_head, 1 - kv_head
        qp = q_ref[0, :, pair * LANES:(pair + 1) * LANES]
        zero = jnp.zeros_like(qp)
        s = jnp.concatenate([_dot_nt(jnp.where(low, qp, zero), k_tiles[lo_src]),
                             _dot_nt(jnp.where(low, zero, qp), k_tiles[hi_src])], axis=0)
        s = jnp.where(valid, s, NEG_INF)
        sk = jnp.where(low_rows, sink_ref[layer, 2 * pair], sink_ref[layer, 2 * pair + 1])
        m = jnp.maximum(jnp.max(s, axis=-1, keepdims=True), sk)
        p = jnp.exp(s - m)
        p = (p / (jnp.sum(p, axis=-1, keepdims=True) + jnp.exp(sk - m))).astype(BF16)
        o_lo = jnp.dot(p[0:blk], v_tiles[lo_src], preferred_element_type=F32)
        o_hi = jnp.dot(p[blk:], v_tiles[hi_src], preferred_element_type=F32)
        o_ref[0, :, pair * LANES:(pair + 1) * LANES] = jnp.where(low, o_lo, o_hi).astype(BF16)


def _window_call(sink, aq, akv, layer):
    bsz, seq, _ = aq.shape
    nb = seq // A_BLOCK
    nq = A_BLOCK * A_QBLOCKS
    assert seq % nq == 0
    kvspec = lambda i: pl.BlockSpec((1, A_BLOCK, 4 * LANES),
                                    lambda b, n: (b, jnp.clip(n * A_QBLOCKS - 1 + i, 0, nb - 1), 0))
    nkv = A_QBLOCKS + 2
    return pl.pallas_call(
        functools.partial(_window_kernel, seq=seq, layer=layer),
        grid=(bsz, seq // nq),
        in_specs=[pl.BlockSpec(memory_space=pltpu.SMEM),
                  pl.BlockSpec((1, nq, A_Q), lambda b, n: (b, n, 0))] + [kvspec(i) for i in range(nkv)],
        out_specs=pl.BlockSpec((1, nq, A_Q), lambda b, n: (b, n, 0)),
        out_shape=jax.ShapeDtypeStruct((bsz, seq, A_Q), BF16),
        compiler_params=_params(("parallel", "arbitrary")),
        name="window_gqa",
    )(sink, aq, *([akv] * nkv))


def _nbr_bias_tables(rpb, rows):
    groups = rows // B_QROWS
    qr = np.arange(B_QROWS)
    kr = np.arange(B_KROWS)
    row_onehot = np.zeros((3, B_QROWS, B_KROWS, 2 * B_KH - 1), np.float32)
    row_valid = np.zeros((3, B_QROWS, B_KROWS), bool)
    for var, g in enumerate((0, 1, groups - 1)):
        base = int(np.clip(g - 1, 0, groups - 3)) * B_QROWS
        r = g * B_QROWS + qr
        rstart = np.clip(r - B_KH // 2, 0, rows - B_KH)
        krow = base + kr
        ok = (krow[None, :] >= rstart[:, None]) & (krow[None, :] < rstart[:, None] + B_KH)
        drow = np.clip(krow[None, :] - r[:, None] + (B_KH - 1), 0, 2 * B_KH - 2)
        row_valid[var] = ok
        row_onehot[var, qr[:, None], kr[None, :], drow] = 1.0
    qc = np.arange(GRID_W)
    kc = np.arange(GRID_W)
    cstart = np.clip(qc - B_KW // 2, 0, GRID_W - B_KW)
    col_valid = (kc[None, :] >= cstart[:, None]) & (kc[None, :] < cstart[:, None] + B_KW)
    dcol = np.clip(kc[None, :] - qc[:, None], 1 - B_KW, B_KW - 1) + (B_KW - 1)
    col_onehot = np.zeros((GRID_W, GRID_W, 2 * B_KW - 1), np.float32)
    col_onehot[qc[:, None], kc[None, :], dcol] = 1.0
    valid = row_valid[:, :, None, :, None] & col_valid[None, None, :, None, :]
    hi = lax.Precision.HIGHEST
    by_col = jnp.einsum('lhab,qkb->lhaqk', rpb, jnp.asarray(col_onehot), precision=hi)
    tiles = jnp.einsum('lhaqk,vrsa->lvhrqsk', by_col, jnp.asarray(row_onehot), precision=hi)
    tiles = jnp.where(jnp.asarray(valid)[None, :, None], tiles, NEG_INF)
    depth = rpb.shape[0]
    return tiles.reshape(depth, 3, B_HEADS, B_QROWS * GRID_W, B_KROWS * GRID_W)


def _nbr_kernel(q_ref, k0_ref, k1_ref, k2_ref, v0_ref, v1_ref, v2_ref, bias_ref, o_ref):
    nq = B_QROWS * GRID_W
    k = jnp.concatenate([k0_ref[0], k1_ref[0], k2_ref[0]], axis=0)
    v = jnp.concatenate([v0_ref[0], v1_ref[0], v2_ref[0]], axis=0)
    low = _low_half_mask()
    for pair in range(B_HEADS // 2):
        cols = slice(pair * LANES, (pair + 1) * LANES)
        qp = q_ref[0, :, cols]
        zero = jnp.zeros_like(qp)
        q2 = jnp.concatenate([jnp.where(low, qp, zero), jnp.where(low, zero, qp)], axis=0)
        bias = jnp.concatenate([bias_ref[0, 0, 2 * pair], bias_ref[0, 0, 2 * pair + 1]], axis=0)
        s = _dot_nt(q2, k[:, cols]) + bias
        tiles = [s[:, c * LANES:(c + 1) * LANES] for c in range(s.shape[1] // LANES)]
        part = tiles[0]
        for t in tiles[1:]:
            part = jnp.maximum(part, t)
        m = jnp.broadcast_to(jnp.max(part, axis=-1, keepdims=True), part.shape)
        p = jnp.concatenate([jnp.exp(t - m) for t in tiles], axis=1).astype(BF16)
        v_ext = jnp.concatenate([v[:, cols], jnp.ones((v.shape[0], LANES), BF16)], axis=1)
        o_ext = jnp.dot(p, v_ext, preferred_element_type=F32)
        o2 = o_ext[:, 0:LANES] / o_ext[:, LANES:]
        o_ref[0, :, cols] = jnp.where(low, o2[0:nq], o2[nq:]).astype(BF16)


def _nbr_call(bq, bk, bv, bias_tiles, layer):
    bsz, seq, w = bq.shape
    nq = B_QROWS * GRID_W
    groups = seq // nq
    start = lambda g: jnp.clip(g - 1, 0, groups - 3)
    kvspec = lambda off: pl.BlockSpec((1, nq, w), lambda b, g: (b, start(g) + off, 0))
    variant = lambda g: jnp.where(g == 0, 0, jnp.where(g == groups - 1, 2, 1))
    return pl.pallas_call(
        _nbr_kernel,
        grid=(bsz, groups),
        in_specs=[pl.BlockSpec((1, nq, w), lambda b, g: (b, g, 0)),
                  kvspec(0), kvspec(1), kvspec(2), kvspec(0), kvspec(1), kvspec(2),
                  pl.BlockSpec((1, 1, B_HEADS, nq, B_KROWS * GRID_W),
                               lambda b, g: (layer, variant(g), 0, 0, 0))],
        out_specs=pl.BlockSpec((1, nq, w), lambda b, g: (b, g, 0)),
        out_shape=jax.ShapeDtypeStruct((bsz, seq, w), BF16),
        compiler_params=_params(("parallel", "arbitrary")),
        name="neighbourhood_attn",
    )(bq, bk, bk, bk, bv, bv, bv, bias_tiles)


def _diff_kernel(lam_ref, ng_ref, q_ref, k_ref, v_ref, o_ref, m_ref, acc_ref, *, lam_init):
    j = pl.program_id(3)
    tq = q_ref.shape[1]
    tk = k_ref.shape[1]

    @pl.when(j == 0)
    def _():
        m_ref[...] = jnp.full(m_ref.shape, -jnp.inf, F32)
        acc_ref[...] = jnp.zeros(acc_ref.shape, F32)

    low = _low_half_mask()
    q = q_ref[0]
    k = k_ref[0]
    zero = jnp.zeros_like(q)
    v_ext = jnp.concatenate([v_ref[0], jnp.ones((tk, LANES), BF16)], axis=1)
    scores = [_dot_nt(qm, k) for qm in (jnp.where(low, q, zero), jnp.where(low, zero, q))]
    for mp, s in enumerate(scores):
        tiles = [s[:, c * LANES:(c + 1) * LANES] for c in range(tk // LANES)]
        part = tiles[0]
        for t in tiles[1:]:
            part = jnp.maximum(part, t)
        m_old = m_ref[mp]
        m_new = jnp.maximum(m_old, jnp.broadcast_to(jnp.max(part, axis=-1, keepdims=True), (tq, LANES)))
        alpha = jnp.exp(m_old - m_new)
        p = jnp.concatenate([jnp.exp(t - m_new) for t in tiles], axis=1).astype(BF16)
        acc_ref[mp] = (jnp.concatenate([alpha, alpha], axis=1) * acc_ref[mp]
                       + jnp.dot(p, v_ext, preferred_element_type=F32))
        m_ref[mp] = m_new

    @pl.when(j == pl.num_programs(3) - 1)
    def _():
        lf = lam_ref[...]
        lam = (jnp.exp(jnp.sum(lf[0:1] * lf[1:2], axis=-1, keepdims=True))
               - jnp.exp(jnp.sum(lf[2:3] * lf[3:4], axis=-1, keepdims=True)) + lam_init)
        a1 = acc_ref[0]
        a2 = acc_ref[1]
        o = a1[:, 0:C_VDIM] / a1[:, C_VDIM:] - lam * (a2[:, 0:C_VDIM] / a2[:, C_VDIM:])
        o = o * lax.rsqrt(jnp.mean(o * o, axis=-1, keepdims=True) + LN_EPS)
        o_ref[0] = (o * ng_ref[...] * (1.0 - lam_init)).astype(BF16)


def _diff_call(lam, norm_g, cq, ck, cv, layer, tq, tk):
    bsz, seq, _ = cq.shape
    lam_init = 0.8 - 0.6 * math.exp(-0.3 * layer)
    return pl.pallas_call(
        functools.partial(_diff_kernel, lam_init=lam_init),
        grid=(bsz, C_HEADS, seq // tq, seq // tk),
        in_specs=[pl.BlockSpec((None, 4, HEAD_DIM), lambda b, h, i, j: (layer, 0, 0)),
                  pl.BlockSpec((None, 1, C_VDIM), lambda b, h, i, j: (layer, 0, h)),
                  pl.BlockSpec((1, tq, LANES), lambda b, h, i, j: (b, i, h)),
                  pl.BlockSpec((1, tk, LANES), lambda b, h, i, j: (b, j, h)),
                  pl.BlockSpec((1, tk, C_VDIM), lambda b, h, i, j: (b, j, h))],
        out_specs=pl.BlockSpec((1, tq, C_VDIM), lambda b, h, i, j: (b, i, h)),
        out_shape=jax.ShapeDtypeStruct((bsz, seq, C_V), BF16),
        scratch_shapes=[pltpu.VMEM((2, tq, LANES), F32), pltpu.VMEM((2, tq, C_VDIM + LANES), F32)],
        compiler_params=_params(("parallel", "parallel", "parallel", "arbitrary")),
        name="diff_attn",
    )(lam, norm_g, cq, ck, cv)


def _lru_kernel(xf_ref, xfp_ref, xfn_ref, xb_ref, xbp_ref, xbn_ref, cw_ref, cb_ref,
                w_ref, bias_ref, lam_ref, hf_ref, hb_ref, ext_ref, carry_ref, sa_ref, sb_ref, *, tc):
    j = pl.program_id(1)
    nc = pl.num_programs(1)
    halo = 8
    left = D_CONV // 2

    @pl.when(j == 0)
    def _():
        carry_ref[...] = jnp.zeros(carry_ref.shape, F32)

    z = -lam_ref[...]
    softplus = jnp.maximum(z, 0.0) + jnp.log1p(jnp.exp(-jnp.abs(z)))

    def gates(d, cur_ref, prev_ref, next_ref, chunk):
        ext_ref[0:halo] = jnp.where(chunk > 0, prev_ref[0], 0.0)
        ext_ref[halo:halo + tc] = cur_ref[0]
        ext_ref[halo + tc:2 * halo + tc] = jnp.where(chunk < nc - 1, next_ref[0], 0.0)
        xc = cb_ref[...]
        for t in range(D_CONV):
            xc = xc + cw_ref[t:t + 1] * ext_ref[pl.ds(halo - left + t, tc), :]
        pre = jnp.dot(xc.astype(BF16), w_ref[d], preferred_element_type=F32) + bias_ref[d:d + 1]
        r = jax.nn.sigmoid(pre[:, 0:D_WIDTH])
        i = jax.nn.sigmoid(pre[:, D_WIDTH:2 * D_WIDTH])
        log_a = (-LRU_C) * r * softplus[d:d + 1]
        a = jnp.exp(log_a)
        y = -jnp.tanh(log_a) * (a * a + 1.0)
        root = jnp.where(y > 0.0, y * lax.rsqrt(y), 0.0)
        return a, root * (i * xc)

    sub = lax.broadcasted_iota(jnp.int32, (1, SUBLANES, 1), 1)
    groups = tc // SUBLANES

    def scan(a, b, reverse, carry_row, h_ref):
        width = a.shape[1]
        a = a.reshape(groups, SUBLANES, width)
        b = b.reshape(groups, SUBLANES, width)
        step = 1
        while step < SUBLANES:
            if reverse:
                keep = sub < SUBLANES - step
                shift = SUBLANES - step
            else:
                keep = sub >= step
                shift = step
            a_sh = jnp.where(keep, pltpu.roll(a, shift, 1), 1.0)
            b_sh = jnp.where(keep, pltpu.roll(b, shift, 1), 0.0)
            b = a * b_sh + b
            a = a * a_sh
            step *= 2
        sa_ref[...] = a.reshape(tc, width)
        sb_ref[...] = b.reshape(tc, width)
        h_in = carry_ref[carry_row:carry_row + 1]
        for g in (range(groups - 1, -1, -1) if reverse else range(groups)):
            rows_g = slice(g * SUBLANES, (g + 1) * SUBLANES)
            h_g = sb_ref[rows_g] + sa_ref[rows_g] * h_in
            h_ref[0, rows_g] = h_g
            h_in = h_g[0:1] if reverse else h_g[SUBLANES - 1:SUBLANES]
        carry_ref[carry_row:carry_row + 1] = h_in

    a, b = gates(0, xf_ref, xfp_ref, xfn_ref, j)
    scan(a, b, False, 0, hf_ref)
    a, b = gates(1, xb_ref, xbp_ref, xbn_ref, nc - 1 - j)
    scan(a, b, True, 1, hb_ref)


def _lru_call(dx, conv_w, conv_b, w_gates, b_gates, lam, layer, tc):
    bsz, seq, w = dx.shape
    nc = seq // tc
    hb = tc // 8
    nh = seq // 8
    cur = lambda f: pl.BlockSpec((1, tc, w), f)
    halo = lambda f: pl.BlockSpec((1, 8, w), f)
    full = lambda a: pl.BlockSpec((None,) + a.shape[1:], lambda b, j: (layer,) + (0,) * (a.ndim - 1))
    mirror = lambda j: nc - 1 - j
    return pl.pallas_call(
        functools.partial(_lru_kernel, tc=tc),
        grid=(bsz, nc),
        in_specs=[cur(lambda b, j: (b, j, 0)),
                  halo(lambda b, j: (b, jnp.maximum(j * hb - 1, 0), 0)),
                  halo(lambda b, j: (b, jnp.minimum((j + 1) * hb, nh - 1), 0)),
                  cur(lambda b, j: (b, mirror(j), 0)),
                  halo(lambda b, j: (b, jnp.maximum(mirror(j) * hb - 1, 0), 0)),
                  halo(lambda b, j: (b, jnp.minimum((mirror(j) + 1) * hb, nh - 1), 0)),
                  full(conv_w), full(conv_b), full(w_gates), full(b_gates), full(lam)],
        out_specs=[cur(lambda b, j: (b, j, 0)), cur(lambda b, j: (b, mirror(j), 0))],
        out_shape=[jax.ShapeDtypeStruct((bsz, seq, w), F32)] * 2,
        scratch_shapes=[pltpu.VMEM((tc + 16, w), F32), pltpu.VMEM((2, w), F32),
                        pltpu.VMEM((tc, w), F32), pltpu.VMEM((tc, w), F32)],
        compiler_params=_params(("parallel", "arbitrary")),
        name="rg_lru",
    )(dx, dx, dx, dx, dx, dx, conv_w, conv_b, w_gates, b_gates, lam)


def _merge_kernel(x_ref, sh1_ref, sc1_ref, g1_ref, sh2_ref, sc2_ref,
                  oa_ref, ob_ref, oc_ref, hf_ref, hb_ref, dg_ref,
                  wg_ref, bg_ref, wbr_ref, wo_ref, lng_ref, lnb_ref,
                  x1_ref, h2_ref):
    x = x_ref[0]
    h = (x * (1.0 + sc1_ref[...]) + sh1_ref[...]).astype(BF16)
    od = ((hf_ref[0] + hb_ref[0]) * _gelu(dg_ref[0])).astype(BF16)
    branches = (oa_ref[0], ob_ref[0], oc_ref[0], od)
    mixed = None
    for n in range(N_BRANCH):
        cols = slice(n * D_MODEL, (n + 1) * D_MODEL)
        gate = jax.nn.sigmoid(jnp.dot(h, wg_ref[:, cols], preferred_element_type=F32) + bg_ref[:, cols])
        term = gate * jnp.dot(branches[n], wbr_ref[n], preferred_element_type=F32)
        mixed = term if mixed is None else mixed + term
    mix = jnp.dot(mixed.astype(BF16), wo_ref[...], preferred_element_type=F32)
    x1 = _layer_norm(ALPHA * x + g1_ref[...] * mix, lng_ref[...], lnb_ref[...])
    x1_ref[0] = x1
    h2_ref[0] = (x1 * (1.0 + sc2_ref[...]) + sh2_ref[...]).astype(BF16)


def _merge_call(x, mod, oa, ob, oc, hf, hb, dg, wg, bg, wbr, wo, lng, lnb, layer, tm):
    bsz, seq, d = x.shape
    tok = lambda w: pl.BlockSpec((1, tm, w), lambda b, i: (b, i, 0))
    full = lambda a: pl.BlockSpec((None,) + a.shape[1:], lambda b, i: (layer,) + (0,) * (a.ndim - 1))
    ln = pl.BlockSpec((None, 1, d), lambda b, i: (2 * layer, 0, 0))
    return pl.pallas_call(
        _merge_kernel,
        grid=(bsz, seq // tm),
        in_specs=[tok(d)] + [_mod_spec(layer, which, bsz, d) for which in range(5)] + [tok(BRANCH_W)] * 6
                 + [full(wg), full(bg), full(wbr), full(wo), ln, ln],
        out_specs=[tok(d), tok(d)],
        out_shape=[jax.ShapeDtypeStruct((bsz, seq, d), F32),
                   jax.ShapeDtypeStruct((bsz, seq, d), BF16)],
        compiler_params=_params(("parallel", "arbitrary")),
        name="merge_out_ln",
    )(x, mod, mod, mod, mod, mod, oa, ob, oc, hf, hb, dg, wg, bg, wbr, wo, lng, lnb)


def _sorting_network(n):
    def merge(lo, hi, r):
        step = r * 2
        if step < hi - lo:
            yield from merge(lo, hi, step)
            yield from merge(lo + r, hi, step)
            yield from ((i, i + r) for i in range(lo + r, hi - r, step))
        else:
            yield (lo, lo + r)

    def sort(lo, hi):
        if hi - lo >= 1:
            mid = lo + (hi - lo) // 2
            yield from sort(lo, mid)
            yield from sort(mid + 1, hi)
            yield from merge(lo, hi, 1)

    return tuple(sort(0, n - 1))


TOP_LISTS = 16
SORT_NET = _sorting_network(TOP_LISTS)


def _top_values(work, count):
    rows, cols = work.shape
    tiles = [work[i * SUBLANES:(i + 1) * SUBLANES] for i in range(rows // SUBLANES)]
    tiles += [jnp.full((SUBLANES, cols), -jnp.inf, work.dtype)] * (TOP_LISTS - len(tiles))
    for i, j in SORT_NET:
        tiles[i], tiles[j] = jnp.maximum(tiles[i], tiles[j]), jnp.minimum(tiles[i], tiles[j])
    vals = []
    for r in range(count):
        head = tiles[0]
        mx = jnp.max(head, axis=0, keepdims=True)
        vals.append(mx)
        advance = head == mx
        for i in range(min(TOP_LISTS, count - r - 1)):
            below = tiles[i + 1] if i + 1 < TOP_LISTS else -jnp.inf
            tiles[i] = jnp.where(advance, below, tiles[i])
    return vals


def _peer_route_kernel(h2_ref, wq_ref, sk_ref, s1_ref, s2_ref, tau_ref):
    q = jnp.dot(h2_ref[...], wq_ref[...], preferred_element_type=F32).astype(BF16)
    k = PEER_TOPK
    pad = jnp.full((SUBLANES - 1, q.shape[0]), -jnp.inf, F32)
    for head in range(PEER_HEADS):
        scores = []
        top = []
        for part in range(2):
            hp = 2 * head + part
            st = _dot_nt(sk_ref[hp], q[:, hp * LANES:(hp + 1) * LANES]) * LOG2_E
            scores.append(st)
            top.append(jnp.concatenate(_top_values(st, k + 1) + [pad], axis=0))
        sv1, sv2 = top
        cand = [sv1[0:1] + sv2]
        cand += [sv1[a:a + 1] + sv2[0:SUBLANES] for a in range(1, SUBLANES)]
        cand += [sv1[SUBLANES:] + sv2[0:1]]
        best = _top_values(jnp.concatenate(cand, axis=0), k + 1)
        m = best[0]
        zsum = jnp.zeros_like(m)
        for t in best[:k]:
            zsum = zsum + jnp.exp2(t - m)
        c = m + jnp.log2(zsum)
        s1_ref[head] = scores[0] - c
        s2_ref[head] = scores[1]
        tau_ref[head:head + 1] = 0.5 * (best[k - 1] + best[k]) - c


def _peer_route_call(h2, wq, subkeys, layer, tm):
    tokens, d = h2.shape
    score_spec = pl.BlockSpec((PEER_HEADS, PEER_KEYS, tm), lambda i: (0, 0, i))
    stat_spec = pl.BlockSpec((PEER_HEADS, tm), lambda i: (0, i))
    score_shape = jax.ShapeDtypeStruct((PEER_HEADS, PEER_KEYS, tokens), F32)
    stat_shape = jax.ShapeDtypeStruct((PEER_HEADS, tokens), F32)
    return pl.pallas_call(
        _peer_route_kernel,
        grid=(tokens // tm,),
        in_specs=[pl.BlockSpec((tm, d), lambda i: (i, 0)),
                  pl.BlockSpec((None,) + wq.shape[1:], lambda i: (layer, 0, 0)),
                  pl.BlockSpec((None,) + subkeys.shape[1:], lambda i: (layer, 0, 0, 0))],
        out_specs=[score_spec, score_spec, stat_spec],
        out_shape=[score_shape, score_shape, stat_shape],
        compiler_params=_params(("parallel",)),
        name="peer_route",
    )(h2, wq, subkeys)


def _peer_mix_kernel(h2_ref, ux_ref, uy_ref, vtx_ref, vty_ref, s1x_ref, s1y_ref, s2_ref, tau_ref,
                     x1_ref, g2_ref, lng_ref, lnb_ref, o_ref,
                     acc_ref, stx_ref, sty_ref, wx_ref, wy_ref, *, tt):
    j = pl.program_id(1)

    @pl.when(j == 0)
    def _():
        acc_ref[...] = jnp.zeros(acc_ref.shape, F32)
        sty_ref[...] = jnp.zeros(sty_ref.shape, F32)
        wx_ref[...] = jnp.zeros(wx_ref.shape, BF16)

    per_tile = SUBLANES // 2
    width = 2 * LANES

    def half_step(u_ref, st_out, st_in, s1_ref, row0, w_out, vt_ref, w_in):
        for c in range(tt // width):
            cols = slice(c * width, (c + 1) * width)
            st_out[:, cols] = _dot_nt(u_ref[...], h2_ref[cols, :])
            for half in range(width // LANES):
                lanes = slice(c * width + half * LANES, c * width + (half + 1) * LANES)
                for a in range(per_tile):
                    g = jnp.zeros((PEER_KEYS, LANES), F32)
                    for head in range(PEER_HEADS):
                        zsum = s1_ref[head, row0 + a:row0 + a + 1, lanes] + s2_ref[head, :, lanes]
                        g = g + jnp.where(zsum >= tau_ref[head:head + 1, lanes], jnp.exp2(zsum), 0.0)
                    rows = slice(a * PEER_KEYS, (a + 1) * PEER_KEYS)
                    w_out[rows, lanes] = _gelu(st_in[rows, lanes].astype(BF16)) * g.astype(BF16)
            acc_ref[:, cols] += jnp.dot(vt_ref[...], w_in[:, cols], preferred_element_type=F32)

    half_step(ux_ref, stx_ref, sty_ref, s1x_ref, per_tile, wy_ref, vtx_ref, wx_ref)
    half_step(uy_ref, sty_ref, stx_ref, s1y_ref, 0, wx_ref, vty_ref, wy_ref)

    @pl.when(j == pl.num_programs(1) - 1)
    def _():
        y = acc_ref[...].T
        o_ref[...] = _layer_norm(ALPHA * x1_ref[...] + g2_ref[...] * y, lng_ref[...], lnb_ref[...])


def _peer_mix_call(h2, u, vt, layer, s1, s2, tau, x1, mod, lng, lnb, seq, tt):
    tokens, d = h2.shape
    bsz = tokens // seq
    tn = (SUBLANES // 2) * PEER_KEYS
    n_tiles = u.shape[1] // tn
    pairs = n_tiles // 2
    per_batch = seq // tt
    assert u.shape[1] % (2 * tn) == 0 and tt % (2 * LANES) == 0
    clamp = lambda t: jnp.clip(t, 0, n_tiles - 1)
    u_spec = lambda f: pl.BlockSpec((None, tn, d), lambda i, j: (layer, clamp(f(j)), 0))
    vt_spec = lambda f: pl.BlockSpec((None, d, tn), lambda i, j: (layer, 0, clamp(f(j))))
    row_spec = lambda f: pl.BlockSpec((PEER_HEADS, SUBLANES, tt), lambda i, j: (0, jnp.clip(f(j), 0, pairs - 1), i))
    key_spec = pl.BlockSpec((PEER_HEADS, PEER_KEYS, tt), lambda i, j: (0, 0, i))
    stat_spec = pl.BlockSpec((PEER_HEADS, tt), lambda i, j: (0, i))
    return pl.pallas_call(
        functools.partial(_peer_mix_kernel, tt=tt),
        grid=(tokens // tt, pairs + 1),
        in_specs=[pl.BlockSpec((tt, d), lambda i, j: (i, 0)),
                  u_spec(lambda j: 2 * j), u_spec(lambda j: 2 * j + 1),
                  vt_spec(lambda j: 2 * j - 2), vt_spec(lambda j: 2 * j - 1),
                  row_spec(lambda j: j - 1), row_spec(lambda j: j),
                  key_spec, stat_spec,
                  pl.BlockSpec((tt, d), lambda i, j: (i, 0)),
                  pl.BlockSpec((None, 1, d), lambda i, j: ((layer * bsz + i // per_batch) * 6 + 5, 0, 0)),
                  pl.BlockSpec((None, 1, d), lambda i, j: (2 * layer + 1, 0, 0)),
                  pl.BlockSpec((None, 1, d), lambda i, j: (2 * layer + 1, 0, 0))],
        out_specs=pl.BlockSpec((tt, d), lambda i, j: (i, 0)),
        out_shape=jax.ShapeDtypeStruct((tokens, d), F32),
        scratch_shapes=[pltpu.VMEM((d, tt), F32),
                        pltpu.VMEM((tn, tt), F32), pltpu.VMEM((tn, tt), F32),
                        pltpu.VMEM((tn, tt), BF16), pltpu.VMEM((tn, tt), BF16)],
        compiler_params=_params(("parallel", "arbitrary")),
        name="peer_mix",
    )(h2, u, u, vt, vt, s1, s1, s2, tau, x1, mod, lng, lnb)


def _rope_tables(seq):
    inv = 1.0 / (ROPE_THETA ** (jnp.arange(0, HEAD_DIM, 2, dtype=F32) / HEAD_DIM))
    ang = jnp.arange(seq, dtype=F32)[:, None] * inv[None, :]
    cos, sin = jnp.cos(ang), jnp.sin(ang)
    zero = jnp.zeros_like(sin)
    reps = LANES // HEAD_DIM
    cos_t = jnp.tile(jnp.concatenate([cos, cos], axis=1), (1, reps))
    sin_lo = jnp.tile(jnp.concatenate([-sin, zero], axis=1), (1, reps))
    sin_hi = jnp.tile(jnp.concatenate([zero, sin], axis=1), (1, reps))
    return cos_t, sin_lo, sin_hi


def _block_diag(w):
    nb, bs, _ = w.shape
    eye = jnp.eye(nb, dtype=w.dtype)
    return (eye[:, None, :, None] * w[:, :, None, :]).reshape(nb * bs, nb * bs)


def _tile(n, want):
    t = min(n, want)
    assert n % t == 0
    return t


def kernel(x, c, w_ada, b_ada, w_in, b_gate, a_sink, b_rpb, c_lambda, c_norm_g, d_conv_w, d_conv_b, d_wa, d_ba, d_wx, d_bx, d_lam, w_branch, w_out, ln_g, ln_b, p_wq, p_subkeys, p_u, p_v):
    bsz, seq, d = x.shape
    tokens = bsz * seq
    rows = seq // GRID_W
    assert d == D_MODEL and seq % (B_QROWS * GRID_W) == 0 and rows // B_QROWS >= 3

    c_pad = jnp.pad(c, ((0, SUBLANES - bsz), (0, 0)))
    mod = _ada_call(c_pad, w_ada, b_ada)[:, :bsz].reshape(DEPTH * bsz * 6, 1, d)
    cos_t, sin_lo, sin_hi = _rope_tables(seq)
    bias_tiles = _nbr_bias_tables(b_rpb, rows)

    w_front = w_in[:, :, :OFF_G].astype(BF16)
    w_gate = w_in[:, :, OFF_G:].astype(BF16)
    b_gate3 = b_gate.reshape(DEPTH, 1, N_BRANCH * d)
    wbr16 = w_branch.astype(BF16)
    wo16 = w_out.astype(BF16)
    ln_g2 = ln_g.reshape(DEPTH * 2, 1, d)
    ln_b2 = ln_b.reshape(DEPTH * 2, 1, d)
    wq16 = p_wq.astype(BF16)
    sk16 = p_subkeys.astype(BF16).reshape(DEPTH, 2 * PEER_HEADS, PEER_KEYS, PEER_QDIM // 2)
    u16 = p_u.astype(BF16)
    vt16 = jnp.swapaxes(p_v, 1, 2).astype(BF16)
    blocks = jax.vmap(jax.vmap(_block_diag))
    lru_w = jnp.concatenate([blocks(d_wa), blocks(d_wx)], axis=-1).astype(BF16)
    lru_b = jnp.concatenate([d_ba, d_bx], axis=-1)
    conv_b = d_conv_b.reshape(DEPTH, 1, D_WIDTH)
    norm_g = c_norm_g.reshape(DEPTH, 1, C_V)

    for l in range(DEPTH):
        aq, akv, bq, bk, bv, cq, ck, cv, dx, dg = _inproj_call(
            x, mod, w_front, l, cos_t, sin_lo, sin_hi, _tile(seq, 512))
        oa = _window_call(a_sink, aq, akv, l)
        ob = _nbr_call(bq, bk, bv, bias_tiles, l)
        oc = _diff_call(c_lambda, norm_g, cq, ck, cv, l, _tile(seq, DIFF_TQ), _tile(seq, DIFF_TK))
        hf, hb = _lru_call(dx, d_conv_w, conv_b, lru_w, lru_b, d_lam, l, _tile(seq, 256))
        x1, h2 = _merge_call(x, mod, oa, ob, oc, hf, hb, dg, w_gate, b_gate3, wbr16, wo16,
                             ln_g2, ln_b2, l, _tile(seq, 256))
        h2f = h2.reshape(tokens, d)
        s1, s2, tau = _peer_route_call(h2f, wq16, sk16, l, _tile(tokens, 512))
        xn = _peer_mix_call(h2f, u16, vt16, l, s1, s2, tau, x1.reshape(tokens, d), mod,
                            ln_g2, ln_b2, seq, _tile(seq, 1024))
        x = xn.reshape(bsz, seq, d)
    return x
```

```python
import functools
import math

import numpy as np
import jax
import jax.numpy as jnp
from jax import lax
from jax.experimental import pallas as pl
from jax.experimental.pallas import tpu as pltpu

F32 = jnp.float32
BF16 = jnp.bfloat16

LANES = 128
SUBLANES = 8
VMEM_LIMIT = 56 * 1024 * 1024

D_MODEL = 1024
DEPTH = 4
GRID_W = 64
HEAD_DIM = 64
ROPE_THETA = 10000.0
NEG_INF = -1e30
LN_EPS = 1e-5
A_HEADS = 8
A_KV_HEADS = 2
A_WINDOW = 128
A_BLOCK = 128
B_HEADS = 8
B_KH = 8
B_KW = 16
C_HEADS = 4
C_VDIM = 2 * HEAD_DIM
D_WIDTH = 512
D_BLOCKS = 8
D_CONV = 4
LRU_C = 8.0
N_BRANCH = 4
BRANCH_W = 512
PEER_HEADS = 8
PEER_KEYS = 128
PEER_N = PEER_KEYS * PEER_KEYS
PEER_QDIM = 256
PEER_TOPK = 16
ALPHA = (2.0 * DEPTH) ** 0.25
LOG2_E = math.log2(math.e)

A_Q = A_HEADS * HEAD_DIM
A_KV = A_KV_HEADS * HEAD_DIM
B_QKV = B_HEADS * HEAD_DIM
C_QK = C_HEADS * 2 * HEAD_DIM
C_V = C_HEADS * C_VDIM
OFF_A = 0
OFF_B = A_Q + 2 * A_KV
OFF_C = OFF_B + 3 * B_QKV
OFF_D = OFF_C + 2 * C_QK + C_V
OFF_G = OFF_D + 2 * D_WIDTH

DIFF_TQ, DIFF_TK = 1024, 2048
A_QBLOCKS = 2

B_QROWS = 4
B_KROWS = B_QROWS + B_KH


def _params(sem):
    return pltpu.CompilerParams(dimension_semantics=sem, vmem_limit_bytes=VMEM_LIMIT)


def _gelu(x):
    c = math.sqrt(2.0 / math.pi)
    half = 0.5 * x
    return half + half * jnp.tanh(x * (c + (c * 0.044715) * (x * x)))


def _layer_norm(y, g, b):
    mu = jnp.mean(y, axis=-1, keepdims=True)
    yc = y - mu
    var = jnp.mean(yc * yc, axis=-1, keepdims=True)
    return yc * lax.rsqrt(var + LN_EPS) * g + b


def _dot_nt(a, b):
    return lax.dot_general(a, b, (((1,), (1,)), ((), ())), preferred_element_type=F32)


def _low_half_mask():
    return lax.broadcasted_iota(jnp.int32, (1, LANES), 1) < HEAD_DIM


def _ada_kernel(c_ref, w_ref, b_ref, o_ref):
    c = c_ref[...]
    ca = c * jax.nn.sigmoid(c)
    o_ref[0] = jnp.dot(ca, w_ref[0], precision=lax.Precision.HIGHEST,
                       preferred_element_type=F32) + b_ref[0]


def _ada_call(c_pad, w_ada, b_ada):
    depth, d, cols = w_ada.shape
    rows = c_pad.shape[0]
    tn = 1024
    return pl.pallas_call(
        _ada_kernel,
        grid=(depth, cols // tn),
        in_specs=[pl.BlockSpec((rows, d), lambda l, j: (0, 0)),
                  pl.BlockSpec((1, d, tn), lambda l, j: (l, 0, j)),
                  pl.BlockSpec((1, 1, tn), lambda l, j: (l, 0, j))],
        out_specs=pl.BlockSpec((1, rows, tn), lambda l, j: (l, 0, j)),
        out_shape=jax.ShapeDtypeStruct((depth, rows, cols), F32),
        compiler_params=_params(("arbitrary", "arbitrary")),
        name="adaln_mod",
    )(c_pad, w_ada, b_ada.reshape(depth, 1, cols))


def _inproj_kernel(x_ref, sh_ref, sc_ref, w_ref, cos_ref, slo_ref, shi_ref,
                   aq_ref, akv_ref, bq_ref, bk_ref, bv_ref, cq_ref, ck_ref, cv_ref,
                   dx_ref, dg_ref):
    h = (x_ref[0] * (1.0 + sc_ref[...]) + sh_ref[...]).astype(BF16)
    cos = cos_ref[...]
    slo = slo_ref[...]
    shi = shi_ref[...]
    qscale = HEAD_DIM ** -0.5

    def rope(v):
        return (v * cos + pltpu.roll(v, LANES - HEAD_DIM // 2, 1) * slo
                + pltpu.roll(v, HEAD_DIM // 2, 1) * shi)

    def tiles(y, start, n):
        return [y[:, (start + i) * LANES:(start + i + 1) * LANES] for i in range(n)]

    ya = jnp.dot(h, w_ref[:, OFF_A:OFF_B], preferred_element_type=F32)
    for i, t in enumerate(tiles(ya, 0, A_Q // LANES)):
        aq_ref[0, :, i * LANES:(i + 1) * LANES] = (rope(t) * qscale).astype(BF16)
    ak = rope(ya[:, A_Q:A_Q + LANES])
    av = ya[:, A_Q + LANES:A_Q + 2 * LANES]
    for i, t in enumerate((ak, av, pltpu.roll(ak, HEAD_DIM, 1), pltpu.roll(av, HEAD_DIM, 1))):
        akv_ref[0, :, i * LANES:(i + 1) * LANES] = t.astype(BF16)

    yb = jnp.dot(h, w_ref[:, OFF_B:OFF_C], preferred_element_type=F32)
    bq_ref[0] = (yb[:, 0:B_QKV] * qscale).astype(BF16)
    bk_ref[0] = yb[:, B_QKV:2 * B_QKV].astype(BF16)
    bv_ref[0] = yb[:, 2 * B_QKV:3 * B_QKV].astype(BF16)

    yc = jnp.dot(h, w_ref[:, OFF_C:OFF_D], preferred_element_type=F32)
    nt = C_QK // LANES
    for i, t in enumerate(tiles(yc, 0, nt)):
        cq_ref[0, :, i * LANES:(i + 1) * LANES] = (rope(t) * qscale).astype(BF16)
    for i, t in enumerate(tiles(yc, nt, nt)):
        ck_ref[0, :, i * LANES:(i + 1) * LANES] = rope(t).astype(BF16)
    cv_ref[0] = yc[:, 2 * C_QK:2 * C_QK + C_V].astype(BF16)

    yd = jnp.dot(h, w_ref[:, OFF_D:OFF_G], preferred_element_type=F32)
    dx_ref[0] = yd[:, 0:D_WIDTH]
    dg_ref[0] = yd[:, D_WIDTH:2 * D_WIDTH]


def _mod_spec(layer, which, bsz, d):
    return pl.BlockSpec((None, 1, d), lambda b, *_: ((layer * bsz + b) * 6 + which, 0, 0))


def _inproj_call(x, mod, w_front, layer, cos_t, sin_lo, sin_hi, tm):
    bsz, seq, d = x.shape
    tok = lambda w: pl.BlockSpec((1, tm, w), lambda b, i: (b, i, 0))
    tab = pl.BlockSpec((tm, LANES), lambda b, i: (i, 0))
    outs = [(A_Q, BF16), (4 * LANES, BF16), (B_QKV, BF16), (B_QKV, BF16), (B_QKV, BF16),
            (C_QK, BF16), (C_QK, BF16), (C_V, BF16), (D_WIDTH, F32), (D_WIDTH, F32)]
    return pl.pallas_call(
        _inproj_kernel,
        grid=(bsz, seq // tm),
        in_specs=[tok(d), _mod_spec(layer, 0, bsz, d), _mod_spec(layer, 1, bsz, d),
                  pl.BlockSpec((None,) + w_front.shape[1:], lambda b, i: (layer, 0, 0)), tab, tab, tab],
        out_specs=[tok(w) for w, _ in outs],
        out_shape=[jax.ShapeDtypeStruct((bsz, seq, w), dt) for w, dt in outs],
        compiler_params=_params(("parallel", "arbitrary")),
        name="in_proj",
    )(x, mod, mod, w_front, cos_t, sin_lo, sin_hi)


def _window_kernel(sink_ref, q_ref, *rest, seq, layer):
    kv_refs, o_ref = rest[:-1], rest[-1]
    n = pl.program_id(1)
    blk = A_BLOCK * A_QBLOCKS
    nk = A_BLOCK * len(kv_refs)
    kv = jnp.concatenate([r[0] for r in kv_refs], axis=0)
    k_tiles = (kv[:, 0:LANES], kv[:, 2 * LANES:3 * LANES])
    v_tiles = (kv[:, LANES:2 * LANES], kv[:, 3 * LANES:4 * LANES])
    qi = lax.broadcasted_iota(jnp.int32, (2 * blk, nk), 0) % blk
    kj = lax.broadcasted_iota(jnp.int32, (2 * blk, nk), 1)
    rel = kj - A_BLOCK - qi
    kpos = n * blk + kj - A_BLOCK
    valid = (jnp.abs(rel) <= A_WINDOW) & (kpos >= 0) & (kpos < seq)
    low_rows = lax.broadcasted_iota(jnp.int32, (2 * blk, 1), 0) < blk
    low = _low_half_mask()
    heads_per_kv = A_HEADS // A_KV_HEADS
    for pair in range(A_HEADS // 2):
        kv_head = (2 * pair) // heads_per_kv
        lo_src, hi_src = kv_head, 1 - kv_head
        qp = q_ref[0, :, pair * LANES:(pair + 1) * LANES]
        zero = jnp.zeros_like(qp)
        s = jnp.concatenate([_dot_nt(jnp.where(low, qp, zero), k_tiles[lo_src]),
                             _dot_nt(jnp.where(low, zero, qp), k_tiles[hi_src])], axis=0)
        s = jnp.where(valid, s, NEG_INF)
        sk = jnp.where(low_rows, sink_ref[layer, 2 * pair], sink_ref[layer, 2 * pair + 1])
        m = jnp.maximum(jnp.max(s, axis=-1, keepdims=True), sk)
        p = jnp.exp(s - m)
        p = (p / (jnp.sum(p, axis=-1, keepdims=True) + jnp.exp(sk - m))).astype(BF16)
        o_lo = jnp.dot(p[0:blk], v_tiles[lo_src], preferred_element_type=F32)
        o_hi = jnp.dot(p[blk:], v_tiles[hi_src], preferred_element_type=F32)
        o_ref[0, :, pair * LANES:(pair + 1) * LANES] = jnp.where(low, o_lo, o_hi).astype(BF16)


def _window_call(sink, aq, akv, layer):
    bsz, seq, _ = aq.shape
    nb = seq // A_BLOCK
    nq = A_BLOCK * A_QBLOCKS
    assert seq % nq == 0
    kvspec = lambda i: pl.BlockSpec((1, A_BLOCK, 4 * LANES),
                                    lambda b, n: (b, jnp.clip(n * A_QBLOCKS - 1 + i, 0, nb - 1), 0))
    nkv = A_QBLOCKS + 2
    return pl.pallas_call(
        functools.partial(_window_kernel, seq=seq, layer=layer),
        grid=(bsz, seq // nq),
        in_specs=[pl.BlockSpec(memory_space=pltpu.SMEM),
                  pl.BlockSpec((1, nq, A_Q), lambda b, n: (b, n, 0))] + [kvspec(i) for i in range(nkv)],
        out_specs=pl.BlockSpec((1, nq, A_Q), lambda b, n: (b, n, 0)),
        out_shape=jax.ShapeDtypeStruct((bsz, seq, A_Q), BF16),
        compiler_params=_params(("parallel", "arbitrary")),
        name="window_gqa",
    )(sink, aq, *([akv] * nkv))


def _nbr_bias_tables(rpb, rows):
    groups = rows // B_QROWS
    qr = np.arange(B_QROWS)
    kr = np.arange(B_KROWS)
    row_onehot = np.zeros((3, B_QROWS, B_KROWS, 2 * B_KH - 1), np.float32)
    row_valid = np.zeros((3, B_QROWS, B_KROWS), bool)
    for var, g in enumerate((0, 1, groups - 1)):
        base = int(np.clip(g - 1, 0, groups - 3)) * B_QROWS
        r = g * B_QROWS + qr
        rstart = np.clip(r - B_KH // 2, 0, rows - B_KH)
        krow = base + kr
        ok = (krow[None, :] >= rstart[:, None]) & (krow[None, :] < rstart[:, None] + B_KH)
        drow = np.clip(krow[None, :] - r[:, None] + (B_KH - 1), 0, 2 * B_KH - 2)
        row_valid[var] = ok
        row_onehot[var, qr[:, None], kr[None, :], drow] = 1.0
    qc = np.arange(GRID_W)
    kc = np.arange(GRID_W)
    cstart = np.clip(qc - B_KW // 2, 0, GRID_W - B_KW)
    col_valid = (kc[None, :] >= cstart[:, None]) & (kc[None, :] < cstart[:, None] + B_KW)
    dcol = np.clip(kc[None, :] - qc[:, None], 1 - B_KW, B_KW - 1) + (B_KW - 1)
    col_onehot = np.zeros((GRID_W, GRID_W, 2 * B_KW - 1), np.float32)
    col_onehot[qc[:, None], kc[None, :], dcol] = 1.0
    valid = row_valid[:, :, None, :, None] & col_valid[None, None, :, None, :]
    hi = lax.Precision.HIGHEST
    by_col = jnp.einsum('lhab,qkb->lhaqk', rpb, jnp.asarray(col_onehot), precision=hi)
    tiles = jnp.einsum('lhaqk,vrsa->lvhrqsk', by_col, jnp.asarray(row_onehot), precision=hi)
    tiles = jnp.where(jnp.asarray(valid)[None, :, None], tiles, NEG_INF)
    depth = rpb.shape[0]
    return tiles.reshape(depth, 3, B_HEADS, B_QROWS * GRID_W, B_KROWS * GRID_W)


def _nbr_kernel(q_ref, k0_ref, k1_ref, k2_ref, v0_ref, v1_ref, v2_ref, bias_ref, o_ref):
    nq = B_QROWS * GRID_W
    k = jnp.concatenate([k0_ref[0], k1_ref[0], k2_ref[0]], axis=0)
    v = jnp.concatenate([v0_ref[0], v1_ref[0], v2_ref[0]], axis=0)
    low = _low_half_mask()
    for pair in range(B_HEADS // 2):
        cols = slice(pair * LANES, (pair + 1) * LANES)
        qp = q_ref[0, :, cols]
        zero = jnp.zeros_like(qp)
        q2 = jnp.concatenate([jnp.where(low, qp, zero), jnp.where(low, zero, qp)], axis=0)
        bias = jnp.concatenate([bias_ref[0, 0, 2 * pair], bias_ref[0, 0, 2 * pair + 1]], axis=0)
        s = _dot_nt(q2, k[:, cols]) + bias
        tiles = [s[:, c * LANES:(c + 1) * LANES] for c in range(s.shape[1] // LANES)]
        part = tiles[0]
        for t in tiles[1:]:
            part = jnp.maximum(part, t)
        m = jnp.broadcast_to(jnp.max(part, axis=-1, keepdims=True), part.shape)
        p = jnp.concatenate([jnp.exp(t - m) for t in tiles], axis=1).astype(BF16)
        v_ext = jnp.concatenate([v[:, cols], jnp.ones((v.shape[0], LANES), BF16)], axis=1)
        o_ext = jnp.dot(p, v_ext, preferred_element_type=F32)
        o2 = o_ext[:, 0:LANES] / o_ext[:, LANES:]
        o_ref[0, :, cols] = jnp.where(low, o2[0:nq], o2[nq:]).astype(BF16)


def _nbr_call(bq, bk, bv, bias_tiles, layer):
    bsz, seq, w = bq.shape
    nq = B_QROWS * GRID_W
    groups = seq // nq
    start = lambda g: jnp.clip(g - 1, 0, groups - 3)
    kvspec = lambda off: pl.BlockSpec((1, nq, w), lambda b, g: (b, start(g) + off, 0))
    variant = lambda g: jnp.where(g == 0, 0, jnp.where(g == groups - 1, 2, 1))
    return pl.pallas_call(
        _nbr_kernel,
        grid=(bsz, groups),
        in_specs=[pl.BlockSpec((1, nq, w), lambda b, g: (b, g, 0)),
                  kvspec(0), kvspec(1), kvspec(2), kvspec(0), kvspec(1), kvspec(2),
                  pl.BlockSpec((1, 1, B_HEADS, nq, B_KROWS * GRID_W),
                               lambda b, g: (layer, variant(g), 0, 0, 0))],
        out_specs=pl.BlockSpec((1, nq, w), lambda b, g: (b, g, 0)),
        out_shape=jax.ShapeDtypeStruct((bsz, seq, w), BF16),
        compiler_params=_params(("parallel", "arbitrary")),
        name="neighbourhood_attn",
    )(bq, bk, bk, bk, bv, bv, bv, bias_tiles)


def _diff_kernel(lam_ref, ng_ref, q_ref, k_ref, v_ref, o_ref, m_ref, acc_ref, *, lam_init):
    j = pl.program_id(3)
    tq = q_ref.shape[1]
    tk = k_ref.shape[1]

    @pl.when(j == 0)
    def _():
        m_ref[...] = jnp.full(m_ref.shape, -jnp.inf, F32)
        acc_ref[...] = jnp.zeros(acc_ref.shape, F32)

    low = _low_half_mask()
    q = q_ref[0]
    k = k_ref[0]
    zero = jnp.zeros_like(q)
    v_ext = jnp.concatenate([v_ref[0], jnp.ones((tk, LANES), BF16)], axis=1)
    scores = [_dot_nt(qm, k) for qm in (jnp.where(low, q, zero), jnp.where(low, zero, q))]
    for mp, s in enumerate(scores):
        tiles = [s[:, c * LANES:(c + 1) * LANES] for c in range(tk // LANES)]
        part = tiles[0]
        for t in tiles[1:]:
            part = jnp.maximum(part, t)
        m_old = m_ref[mp]
        m_new = jnp.maximum(m_old, jnp.broadcast_to(jnp.max(part, axis=-1, keepdims=True), (tq, LANES)))
        alpha = jnp.exp(m_old - m_new)
        p = jnp.concatenate([jnp.exp(t - m_new) for t in tiles], axis=1).astype(BF16)
        acc_ref[mp] = (jnp.concatenate([alpha, alpha], axis=1) * acc_ref[mp]
                       + jnp.dot(p, v_ext, preferred_element_type=F32))
        m_ref[mp] = m_new

    @pl.when(j == pl.num_programs(3) - 1)
    def _():
        lf = lam_ref[...]
        lam = (jnp.exp(jnp.sum(lf[0:1] * lf[1:2], axis=-1, keepdims=True))
               - jnp.exp(jnp.sum(lf[2:3] * lf[3:4], axis=-1, keepdims=True)) + lam_init)
        a1 = acc_ref[0]
        a2 = acc_ref[1]
        o = a1[:, 0:C_VDIM] / a1[:, C_VDIM:] - lam * (a2[:, 0:C_VDIM] / a2[:, C_VDIM:])
        o = o * lax.rsqrt(jnp.mean(o * o, axis=-1, keepdims=True) + LN_EPS)
        o_ref[0] = (o * ng_ref[...] * (1.0 - lam_init)).astype(BF16)


def _diff_call(lam, norm_g, cq, ck, cv, layer, tq, tk):
    bsz, seq, _ = cq.shape
    lam_init = 0.8 - 0.6 * math.exp(-0.3 * layer)
    return pl.pallas_call(
        functools.partial(_diff_kernel, lam_init=lam_init),
        grid=(bsz, C_HEADS, seq // tq, seq // tk),
        in_specs=[pl.BlockSpec((None, 4, HEAD_DIM), lambda b, h, i, j: (layer, 0, 0)),
                  pl.BlockSpec((None, 1, C_VDIM), lambda b, h, i, j: (layer, 0, h)),
                  pl.BlockSpec((1, tq, LANES), lambda b, h, i, j: (b, i, h)),
                  pl.BlockSpec((1, tk, LANES), lambda b, h, i, j: (b, j, h)),
                  pl.BlockSpec((1, tk, C_VDIM), lambda b, h, i, j: (b, j, h))],
        out_specs=pl.BlockSpec((1, tq, C_VDIM), lambda b, h, i, j: (b, i, h)),
        out_shape=jax.ShapeDtypeStruct((bsz, seq, C_V), BF16),
        scratch_shapes=[pltpu.VMEM((2, tq, LANES), F32), pltpu.VMEM((2, tq, C_VDIM + LANES), F32)],
        compiler_params=_params(("parallel", "parallel", "parallel", "arbitrary")),
        name="diff_attn",
    )(lam, norm_g, cq, ck, cv)


def _lru_kernel(xf_ref, xfp_ref, xfn_ref, xb_ref, xbp_ref, xbn_ref, cw_ref, cb_ref,
                w_ref, bias_ref, lam_ref, hf_ref, hb_ref, ext_ref, carry_ref, sa_ref, sb_ref, *, tc):
    j = pl.program_id(1)
    nc = pl.num_programs(1)
    halo = 8
    left = D_CONV // 2

    @pl.when(j == 0)
    def _():
        carry_ref[...] = jnp.zeros(carry_ref.shape, F32)

    z = -lam_ref[...]
    softplus = jnp.maximum(z, 0.0) + jnp.log1p(jnp.exp(-jnp.abs(z)))

    def gates(d, cur_ref, prev_ref, next_ref, chunk):
        ext_ref[0:halo] = jnp.where(chunk > 0, prev_ref[0], 0.0)
        ext_ref[halo:halo + tc] = cur_ref[0]
        ext_ref[halo + tc:2 * halo + tc] = jnp.where(chunk < nc - 1, next_ref[0], 0.0)
        xc = cb_ref[...]
        for t in range(D_CONV):
            xc = xc + cw_ref[t:t + 1] * ext_ref[pl.ds(halo - left + t, tc), :]
        pre = jnp.dot(xc.astype(BF16), w_ref[d], preferred_element_type=F32) + bias_ref[d:d + 1]
        r = jax.nn.sigmoid(pre[:, 0:D_WIDTH])
        i = jax.nn.sigmoid(pre[:, D_WIDTH:2 * D_WIDTH])
        log_a = (-LRU_C) * r * softplus[d:d + 1]
        a = jnp.exp(log_a)
        y = -jnp.tanh(log_a) * (a * a + 1.0)
        root = jnp.where(y > 0.0, y * lax.rsqrt(y), 0.0)
        return a, root * (i * xc)

    sub = lax.broadcasted_iota(jnp.int32, (1, SUBLANES, 1), 1)
    groups = tc // SUBLANES

    def scan(a, b, reverse, carry_row, h_ref):
        width = a.shape[1]
        a = a.reshape(groups, SUBLANES, width)
        b = b.reshape(groups, SUBLANES, width)
        step = 1
        while step < SUBLANES:
            if reverse:
                keep = sub < SUBLANES - step
                shift = SUBLANES - step
            else:
                keep = sub >= step
                shift = step
            a_sh = jnp.where(keep, pltpu.roll(a, shift, 1), 1.0)
            b_sh = jnp.where(keep, pltpu.roll(b, shift, 1), 0.0)
            b = a * b_sh + b
            a = a * a_sh
            step *= 2
        sa_ref[...] = a.reshape(tc, width)
        sb_ref[...] = b.reshape(tc, width)
        h_in = carry_ref[carry_row:carry_row + 1]
        for g in (range(groups - 1, -1, -1) if reverse else range(groups)):
            rows_g = slice(g * SUBLANES, (g + 1) * SUBLANES)
            h_g = sb_ref[rows_g] + sa_ref[rows_g] * h_in
            h_ref[0, rows_g] = h_g
            h_in = h_g[0:1] if reverse else h_g[SUBLANES - 1:SUBLANES]
        carry_ref[carry_row:carry_row + 1] = h_in

    a, b = gates(0, xf_ref, xfp_ref, xfn_ref, j)
    scan(a, b, False, 0, hf_ref)
    a, b = gates(1, xb_ref, xbp_ref, xbn_ref, nc - 1 - j)
    scan(a, b, True, 1, hb_ref)


def _lru_call(dx, conv_w, conv_b, w_gates, b_gates, lam, layer, tc):
    bsz, seq, w = dx.shape
    nc = seq // tc
    hb = tc // 8
    nh = seq // 8
    cur = lambda f: pl.BlockSpec((1, tc, w), f)
    halo = lambda f: pl.BlockSpec((1, 8, w), f)
    full = lambda a: pl.BlockSpec((None,) + a.shape[1:], lambda b, j: (layer,) + (0,) * (a.ndim - 1))
    mirror = lambda j: nc - 1 - j
    return pl.pallas_call(
        functools.partial(_lru_kernel, tc=tc),
        grid=(bsz, nc),
        in_specs=[cur(lambda b, j: (b, j, 0)),
                  halo(lambda b, j: (b, jnp.maximum(j * hb - 1, 0), 0)),
                  halo(lambda b, j: (b, jnp.minimum((j + 1) * hb, nh - 1), 0)),
                  cur(lambda b, j: (b, mirror(j), 0)),
                  halo(lambda b, j: (b, jnp.maximum(mirror(j) * hb - 1, 0), 0)),
                  halo(lambda b, j: (b, jnp.minimum((mirror(j) + 1) * hb, nh - 1), 0)),
                  full(conv_w), full(conv_b), full(w_gates), full(b_gates), full(lam)],
        out_specs=[cur(lambda b, j: (b, j, 0)), cur(lambda b, j: (b, mirror(j), 0))],
        out_shape=[jax.ShapeDtypeStruct((bsz, seq, w), F32)] * 2,
        scratch_shapes=[pltpu.VMEM((tc + 16, w), F32), pltpu.VMEM((2, w), F32),
                        pltpu.VMEM((tc, w), F32), pltpu.VMEM((tc, w), F32)],
        compiler_params=_params(("parallel", "arbitrary")),
        name="rg_lru",
    )(dx, dx, dx, dx, dx, dx, conv_w, conv_b, w_gates, b_gates, lam)


def _merge_kernel(x_ref, sh1_ref, sc1_ref, g1_ref, sh2_ref, sc2_ref,
                  oa_ref, ob_ref, oc_ref, hf_ref, hb_ref, dg_ref,
                  wg_ref, bg_ref, wbr_ref, wo_ref, lng_ref, lnb_ref,
                  x1_ref, h2_ref):
    x = x_ref[0]
    h = (x * (1.0 + sc1_ref[...]) + sh1_ref[...]).astype(BF16)
    od = ((hf_ref[0] + hb_ref[0]) * _gelu(dg_ref[0])).astype(BF16)
    branches = (oa_ref[0], ob_ref[0], oc_ref[0], od)
    mixed = None
    for n in range(N_BRANCH):
        cols = slice(n * D_MODEL, (n + 1) * D_MODEL)
        gate = jax.nn.sigmoid(jnp.dot(h, wg_ref[:, cols], preferred_element_type=F32) + bg_ref[:, cols])
        term = gate * jnp.dot(branches[n], wbr_ref[n], preferred_element_type=F32)
        mixed = term if mixed is None else mixed + term
    mix = jnp.dot(mixed.astype(BF16), wo_ref[...], preferred_element_type=F32)
    x1 = _layer_norm(ALPHA * x + g1_ref[...] * mix, lng_ref[...], lnb_ref[...])
    x1_ref[0] = x1
    h2_ref[0] = (x1 * (1.0 + sc2_ref[...]) + sh2_ref[...]).astype(BF16)


def _merge_call(x, mod, oa, ob, oc, hf, hb, dg, wg, bg, wbr, wo, lng, lnb, layer, tm):
    bsz, seq, d = x.shape
    tok = lambda w: pl.BlockSpec((1, tm, w), lambda b, i: (b, i, 0))
    full = lambda a: pl.BlockSpec((None,) + a.shape[1:], lambda b, i: (layer,) + (0,) * (a.ndim - 1))
    ln = pl.BlockSpec((None, 1, d), lambda b, i: (2 * layer, 0, 0))
    return pl.pallas_call(
        _merge_kernel,
        grid=(bsz, seq // tm),
        in_specs=[tok(d)] + [_mod_spec(layer, which, bsz, d) for which in range(5)] + [tok(BRANCH_W)] * 6
                 + [full(wg), full(bg), full(wbr), full(wo), ln, ln],
        out_specs=[tok(d), tok(d)],
        out_shape=[jax.ShapeDtypeStruct((bsz, seq, d), F32),
                   jax.ShapeDtypeStruct((bsz, seq, d), BF16)],
        compiler_params=_params(("parallel", "arbitrary")),
        name="merge_out_ln",
    )(x, mod, mod, mod, mod, mod, oa, ob, oc, hf, hb, dg, wg, bg, wbr, wo, lng, lnb)


def _sorting_network(n):
    def merge(lo, hi, r):
        step = r * 2
        if step < hi - lo:
            yield from merge(lo, hi, step)
            yield from merge(lo + r, hi, step)
            yield from ((i, i + r) for i in range(lo + r, hi - r, step))
        else:
            yield (lo, lo + r)

    def sort(lo, hi):
        if hi - lo >= 1:
            mid = lo + (hi - lo) // 2
            yield from sort(lo, mid)
            yield from sort(mid + 1, hi)
            yield from merge(lo, hi, 1)

    return tuple(sort(0, n - 1))


TOP_LISTS = 16
SORT_NET = _sorting_network(TOP_LISTS)


def _top_values(work, count):
    rows, cols = work.shape
    tiles = [work[i * SUBLANES:(i + 1) * SUBLANES] for i in range(rows // SUBLANES)]
    tiles += [jnp.full((SUBLANES, cols), -jnp.inf, work.dtype)] * (TOP_LISTS - len(tiles))
    for i, j in SORT_NET:
        tiles[i], tiles[j] = jnp.maximum(tiles[i], tiles[j]), jnp.minimum(tiles[i], tiles[j])
    vals = []
    for r in range(count):
        head = tiles[0]
        mx = jnp.max(head, axis=0, keepdims=True)
        vals.append(mx)
        advance = head == mx
        for i in range(min(TOP_LISTS, count - r - 1)):
            below = tiles[i + 1] if i + 1 < TOP_LISTS else -jnp.inf
            tiles[i] = jnp.where(advance, below, tiles[i])
    return vals


def _peer_route_kernel(h2_ref, wq_ref, sk_ref, s1_ref, s2_ref, tau_ref):
    q = jnp.dot(h2_ref[...], wq_ref[...], preferred_element_type=F32).astype(BF16)
    k = PEER_TOPK
    pad = jnp.full((SUBLANES - 1, q.shape[0]), -jnp.inf, F32)
    for head in range(PEER_HEADS):
        scores = []
        top = []
        for part in range(2):
            hp = 2 * head + part
            st = _dot_nt(sk_ref[hp], q[:, hp * LANES:(hp + 1) * LANES]) * LOG2_E
            scores.append(st)
            top.append(jnp.concatenate(_top_values(st, k + 1) + [pad], axis=0))
        sv1, sv2 = top
        cand = [sv1[0:1] + sv2]
        cand += [sv1[a:a + 1] + sv2[0:SUBLANES] for a in range(1, SUBLANES)]
        cand += [sv1[SUBLANES:] + sv2[0:1]]
        best = _top_values(jnp.concatenate(cand, axis=0), k + 1)
        m = best[0]
        zsum = jnp.zeros_like(m)
        for t in best[:k]:
            zsum = zsum + jnp.exp2(t - m)
        c = m + jnp.log2(zsum)
        s1_ref[head] = scores[0] - c
        s2_ref[head] = scores[1]
        tau_ref[head:head + 1] = 0.5 * (best[k - 1] + best[k]) - c


def _peer_route_call(h2, wq, subkeys, layer, tm):
    tokens, d = h2.shape
    score_spec = pl.BlockSpec((PEER_HEADS, PEER_KEYS, tm), lambda i: (0, 0, i))
    stat_spec = pl.BlockSpec((PEER_HEADS, tm), lambda i: (0, i))
    score_shape = jax.ShapeDtypeStruct((PEER_HEADS, PEER_KEYS, tokens), F32)
    stat_shape = jax.ShapeDtypeStruct((PEER_HEADS, tokens), F32)
    return pl.pallas_call(
        _peer_route_kernel,
        grid=(tokens // tm,),
        in_specs=[pl.BlockSpec((tm, d), lambda i: (i, 0)),
                  pl.BlockSpec((None,) + wq.shape[1:], lambda i: (layer, 0, 0)),
                  pl.BlockSpec((None,) + subkeys.shape[1:], lambda i: (layer, 0, 0, 0))],
        out_specs=[score_spec, score_spec, stat_spec],
        out_shape=[score_shape, score_shape, stat_shape],
        compiler_params=_params(("parallel",)),
        name="peer_route",
    )(h2, wq, subkeys)


def _peer_mix_kernel(h2_ref, ux_ref, uy_ref, vtx_ref, vty_ref, s1x_ref, s1y_ref, s2_ref, tau_ref,
                     x1_ref, g2_ref, lng_ref, lnb_ref, o_ref,
                     acc_ref, stx_ref, sty_ref, wx_ref, wy_ref, *, tt):
    j = pl.program_id(1)
    last = pl.num_programs(1) - 1
    per_tile = SUBLANES // 2
    width = 2 * LANES

    def stage_a(u_ref, st_out, cols):
        st_out[:, cols] = _dot_nt(u_ref[...], h2_ref[cols, :])

    def stage_b(st_in, s1_ref, row0, w_out, lanes):
        for a in range(per_tile):
            g = jnp.zeros((PEER_KEYS, LANES), F32)
            for head in range(PEER_HEADS):
                zsum = s1_ref[head, row0 + a:row0 + a + 1, lanes] + s2_ref[head, :, lanes]
                g = g + jnp.where(zsum >= tau_ref[head:head + 1, lanes], jnp.exp2(zsum), 0.0)
            rows = slice(a * PEER_KEYS, (a + 1) * PEER_KEYS)
            w_out[rows, lanes] = _gelu(st_in[rows, lanes].astype(BF16)) * g.astype(BF16)

    def stage_c(vt_ref, w_in, cols):
        acc_ref[:, cols] += jnp.dot(vt_ref[...], w_in[:, cols], preferred_element_type=F32)

    def half_step(u_ref, st_out, st_in, s1_ref, row0, w_out, vt_ref, w_in, run_a=True, run_b=True, run_c=True):
        for c in range(tt // width):
            cols = slice(c * width, (c + 1) * width)
            if run_a:
                stage_a(u_ref, st_out, cols)
            if run_b:
                for half in range(width // LANES):
                    stage_b(st_in, s1_ref, row0, w_out,
                            slice(c * width + half * LANES, c * width + (half + 1) * LANES))
            if run_c:
                stage_c(vt_ref, w_in, cols)

    @pl.when(j == 0)
    def _():
        acc_ref[...] = jnp.zeros(acc_ref.shape, F32)
        half_step(ux_ref, stx_ref, sty_ref, s1x_ref, per_tile, wy_ref, vtx_ref, wx_ref, run_b=False, run_c=False)
        half_step(uy_ref, sty_ref, stx_ref, s1y_ref, 0, wx_ref, vty_ref, wy_ref, run_c=False)

    @pl.when(jnp.logical_and(j > 0, j < last))
    def _():
        half_step(ux_ref, stx_ref, sty_ref, s1x_ref, per_tile, wy_ref, vtx_ref, wx_ref)
        half_step(uy_ref, sty_ref, stx_ref, s1y_ref, 0, wx_ref, vty_ref, wy_ref)

    @pl.when(j == last)
    def _():
        half_step(ux_ref, stx_ref, sty_ref, s1x_ref, per_tile, wy_ref, vtx_ref, wx_ref, run_a=False)
        half_step(uy_ref, sty_ref, stx_ref, s1y_ref, 0, wx_ref, vty_ref, wy_ref, run_a=False, run_b=False)

    @pl.when(j == last)
    def _():
        y = acc_ref[...].T
        o_ref[...] = _layer_norm(ALPHA * x1_ref[...] + g2_ref[...] * y, lng_ref[...], lnb_ref[...])


def _peer_mix_call(h2, u, vt, layer, s1, s2, tau, x1, mod, lng, lnb, seq, tt):
    tokens, d = h2.shape
    bsz = tokens // seq
    tn = (SUBLANES // 2) * PEER_KEYS
    n_tiles = u.shape[1] // tn
    pairs = n_tiles // 2
    per_batch = seq // tt
    assert u.shape[1] % (2 * tn) == 0 and tt % (2 * LANES) == 0
    clamp = lambda t: jnp.clip(t, 0, n_tiles - 1)
    u_spec = lambda f: pl.BlockSpec((None, tn, d), lambda i, j: (layer, clamp(f(j)), 0))
    vt_spec = lambda f: pl.BlockSpec((None, d, tn), lambda i, j: (layer, 0, clamp(f(j))))
    row_spec = lambda f: pl.BlockSpec((PEER_HEADS, SUBLANES, tt), lambda i, j: (0, jnp.clip(f(j), 0, pairs - 1), i))
    key_spec = pl.BlockSpec((PEER_HEADS, PEER_KEYS, tt), lambda i, j: (0, 0, i))
    stat_spec = pl.BlockSpec((PEER_HEADS, tt), lambda i, j: (0, i))
    return pl.pallas_call(
        functools.partial(_peer_mix_kernel, tt=tt),
        grid=(tokens // tt, pairs + 1),
        in_specs=[pl.BlockSpec((tt, d), lambda i, j: (i, 0)),
                  u_spec(lambda j: 2 * j), u_spec(lambda j: 2 * j + 1),
                  vt_spec(lambda j: 2 * j - 2), vt_spec(lambda j: 2 * j - 1),
                  row_spec(lambda j: j - 1), row_spec(lambda j: j),
                  key_spec, stat_spec,
                  pl.BlockSpec((tt, d), lambda i, j: (i, 0)),
                  pl.BlockSpec((None, 1, d), lambda i, j: ((layer * bsz + i // per_batch) * 6 + 5, 0, 0)),
                  pl.BlockSpec((None, 1, d), lambda i, j: (2 * layer + 1, 0, 0)),
                  pl.BlockSpec((None, 1, d), lambda i, j: (2 * layer + 1, 0, 0))],
        out_specs=pl.BlockSpec((tt, d), lambda i, j: (i, 0)),
        out_shape=jax.ShapeDtypeStruct((tokens, d), F32),
        scratch_shapes=[pltpu.VMEM((d, tt), F32),
                        pltpu.VMEM((tn, tt), F32), pltpu.VMEM((tn, tt), F32),
                        pltpu.VMEM((tn, tt), BF16), pltpu.VMEM((tn, tt), BF16)],
        compiler_params=_params(("parallel", "arbitrary")),
        name="peer_mix",
    )(h2, u, u, vt, vt, s1, s1, s2, tau, x1, mod, lng, lnb)


def _rope_tables(seq):
    inv = 1.0 / (ROPE_THETA ** (jnp.arange(0, HEAD_DIM, 2, dtype=F32) / HEAD_DIM))
    ang = jnp.arange(seq, dtype=F32)[:, None] * inv[None, :]
    cos, sin = jnp.cos(ang), jnp.sin(ang)
    zero = jnp.zeros_like(sin)
    reps = LANES // HEAD_DIM
    cos_t = jnp.tile(jnp.concatenate([cos, cos], axis=1), (1, reps))
    sin_lo = jnp.tile(jnp.concatenate([-sin, zero], axis=1), (1, reps))
    sin_hi = jnp.tile(jnp.concatenate([zero, sin], axis=1), (1, reps))
    return cos_t, sin_lo, sin_hi


def _block_diag(w):
    nb, bs, _ = w.shape
    eye = jnp.eye(nb, dtype=w.dtype)
    return (eye[:, None, :, None] * w[:, :, None, :]).reshape(nb * bs, nb * bs)


def _tile(n, want):
    t = min(n, want)
    assert n % t == 0
    return t


def kernel(x, c, w_ada, b_ada, w_in, b_gate, a_sink, b_rpb, c_lambda, c_norm_g, d_conv_w, d_conv_b, d_wa, d_ba, d_wx, d_bx, d_lam, w_branch, w_out, ln_g, ln_b, p_wq, p_subkeys, p_u, p_v):
    bsz, seq, d = x.shape
    tokens = bsz * seq
    rows = seq // GRID_W
    assert d == D_MODEL and seq % (B_QROWS * GRID_W) == 0 and rows // B_QROWS >= 3

    c_pad = jnp.pad(c, ((0, SUBLANES - bsz), (0, 0)))
    mod = _ada_call(c_pad, w_ada, b_ada)[:, :bsz].reshape(DEPTH * bsz * 6, 1, d)
    cos_t, sin_lo, sin_hi = _rope_tables(seq)
    bias_tiles = _nbr_bias_tables(b_rpb, rows)

    w_front = w_in[:, :, :OFF_G].astype(BF16)
    w_gate = w_in[:, :, OFF_G:].astype(BF16)
    b_gate3 = b_gate.reshape(DEPTH, 1, N_BRANCH * d)
    wbr16 = w_branch.astype(BF16)
    wo16 = w_out.astype(BF16)
    ln_g2 = ln_g.reshape(DEPTH * 2, 1, d)
    ln_b2 = ln_b.reshape(DEPTH * 2, 1, d)
    wq16 = p_wq.astype(BF16)
    sk16 = p_subkeys.astype(BF16).reshape(DEPTH, 2 * PEER_HEADS, PEER_KEYS, PEER_QDIM // 2)
    u16 = p_u.astype(BF16)
    vt16 = jnp.swapaxes(p_v, 1, 2).astype(BF16)
    blocks = jax.vmap(jax.vmap(_block_diag))
    lru_w = jnp.concatenate([blocks(d_wa), blocks(d_wx)], axis=-1).astype(BF16)
    lru_b = jnp.concatenate([d_ba, d_bx], axis=-1)
    conv_b = d_conv_b.reshape(DEPTH, 1, D_WIDTH)
    norm_g = c_norm_g.reshape(DEPTH, 1, C_V)

    for l in range(DEPTH):
        aq, akv, bq, bk, bv, cq, ck, cv, dx, dg = _inproj_call(
            x, mod, w_front, l, cos_t, sin_lo, sin_hi, _tile(seq, 512))
        oa = _window_call(a_sink, aq, akv, l)
        ob = _nbr_call(bq, bk, bv, bias_tiles, l)
        oc = _diff_call(c_lambda, norm_g, cq, ck, cv, l, _tile(seq, DIFF_TQ), _tile(seq, DIFF_TK))
        hf, hb = _lru_call(dx, d_conv_w, conv_b, lru_w, lru_b, d_lam, l, _tile(seq, 256))
        x1, h2 = _merge_call(x, mod, oa, ob, oc, hf, hb, dg, w_gate, b_gate3, wbr16, wo16,
                             ln_g2, ln_b2, l, _tile(seq, 256))
        h2f = h2.reshape(tokens, d)
        s1, s2, tau = _peer_route_call(h2f, wq16, sk16, l, _tile(tokens, 512))
        xn = _peer_mix_call(h2f, u16, vt16, l, s1, s2, tau, x1.reshape(tokens, d), mod,
                            ln_g2, ln_b2, seq, _tile(seq, 1024))
        x = xn.reshape(bsz, seq, d)
    return x
```
